```python
import jax, jax.numpy as jnp
from jax import lax
import numpy as np

D_MODEL = 1024
BATCH = 1
SEQ = 16384
DEPTH = 1

MIX_W = D_MODEL
HEAD_DIM = 64
ATTN_W = MIX_W // 2
N_ATTN_HEADS = ATTN_W // HEAD_DIM
N_KV_HEADS = 2
KV_W = N_KV_HEADS * HEAD_DIM
WINDOW = 128
BLOCK = 128
RET_W = MIX_W - ATTN_W
N_RET_HEADS = 4
RET_HEAD_DIM = RET_W // N_RET_HEADS
RET_CHUNK = 128
IN_W = ATTN_W + 2 * KV_W + 4 * RET_W
D_FF = 2816
CONV_WIDTH = 3
RMS_EPS = 1e-6
GN_EPS = 1e-6
MASK_VALUE = -1e30

kernel_name = "hymba_swa_sink_retention_convffn_sandwich"


def rms_norm(x, w):
    xf = x.astype(jnp.float32)
    y = xf * lax.rsqrt(jnp.mean(xf * xf, axis=-1, keepdims=True) + RMS_EPS)
    return (y * w.astype(jnp.float32)).astype(x.dtype)


def sliding_window_sink_attention(q, k, v, sinks):
    b, s, _ = q.shape
    nb = s // BLOCK
    g = N_ATTN_HEADS // N_KV_HEADS
    qb = q.reshape(b, nb, BLOCK, N_KV_HEADS, g, HEAD_DIM)
    kb = k.reshape(b, nb, BLOCK, N_KV_HEADS, HEAD_DIM)
    vb = v.reshape(b, nb, BLOCK, N_KV_HEADS, HEAD_DIM)
    pad = ((0, 0), (1, 0), (0, 0), (0, 0), (0, 0))
    kk = jnp.concatenate([jnp.pad(kb, pad)[:, :-1], kb], axis=2)
    vv = jnp.concatenate([jnp.pad(vb, pad)[:, :-1], vb], axis=2)
    scores = jnp.einsum('bnqhgd,bnkhd->bnhgqk', qb, kk).astype(jnp.float32) * (HEAD_DIM ** -0.5)
    qpos = jnp.arange(BLOCK)[:, None] + BLOCK
    kpos = jnp.arange(2 * BLOCK)[None, :]
    rel = qpos - kpos
    band = (rel >= 0) & (rel < WINDOW)
    not_pad = (jnp.arange(nb)[:, None, None] > 0) | (kpos >= BLOCK)[None]
    valid = band[None] & not_pad
    scores = jnp.where(valid[None, :, None, None], scores, MASK_VALUE)
    sink = jnp.broadcast_to(
        sinks.astype(jnp.float32).reshape(N_KV_HEADS, g)[None, None, :, :, None, None],
        scores.shape[:-1] + (1,))
    probs = jax.nn.softmax(jnp.concatenate([scores, sink], axis=-1), axis=-1)[..., :-1]
    out = jnp.einsum('bnhgqk,bnkhd->bnqhgd', probs.astype(vv.dtype), vv)
    return out.reshape(b, s, ATTN_W)


def rotate_every_two(x):
    x1 = x[..., ::2]
    x2 = x[..., 1::2]
    return jnp.stack([-x2, x1], axis=-1).reshape(x.shape)


def retention_chunkwise(q, k, v):
    b, s, h, dk = q.shape
    dv = v.shape[-1]
    nc = s // RET_CHUNK
    c = RET_CHUNK
    log_gamma = jnp.log(1.0 - jnp.power(2.0, -5.0 - jnp.arange(h, dtype=jnp.float32)))
    idx = jnp.arange(c, dtype=jnp.float32)
    rel = idx[:, None] - idx[None, :]
    d_intra = jnp.where(rel[None] >= 0,
                        jnp.exp(log_gamma[:, None, None] * jnp.maximum(rel, 0.0)[None]), 0.0)
    xi = jnp.exp(log_gamma[None, :] * (idx[:, None] + 1.0))
    zeta = jnp.exp(log_gamma[None, :] * (c - 1.0 - idx[:, None]))
    chunk_decay = jnp.exp(log_gamma * c)
    qc = q.reshape(b, nc, c, h, dk)
    kc = k.reshape(b, nc, c, h, dk)
    vc = v.reshape(b, nc, c, h, dv)
    inner = jnp.einsum('bnqhd,bnkhd->bnhqk', qc, kc) * d_intra[None, None]
    o_inner = jnp.einsum('bnhqk,bnkhe->bnqhe', inner, vc)
    kv_chunk = jnp.einsum('bnkhd,kh,bnkhe->bnhde', kc, zeta, vc)

    def step(state, kv):
        return chunk_decay[None, :, None, None] * state + kv, state

    _, prev = lax.scan(step, jnp.zeros((b, h, dk, dv), jnp.float32), jnp.moveaxis(kv_chunk, 1, 0))
    prev = jnp.moveaxis(prev, 0, 1)
    o_cross = jnp.einsum('bnqhd,bnhde->bnqhe', qc, prev) * xi[None, None, :, :, None]
    return (o_inner + o_cross).reshape(b, s, h, dv)


def retention_group(q, k, v, gate):
    b, s, _ = q.shape
    dtype = q.dtype
    pos = jnp.arange(s, dtype=jnp.float32)
    angle = 1.0 / jnp.power(10000.0, jnp.linspace(0.0, 1.0, RET_HEAD_DIM // 2, dtype=jnp.float32))
    angle = jnp.repeat(angle, 2)
    sin = jnp.sin(pos[:, None] * angle[None])[None, :, None, :]
    cos = jnp.cos(pos[:, None] * angle[None])[None, :, None, :]
    qf = q.astype(jnp.float32).reshape(b, s, N_RET_HEADS, RET_HEAD_DIM)
    kf = k.astype(jnp.float32).reshape(b, s, N_RET_HEADS, RET_HEAD_DIM) * (RET_HEAD_DIM ** -0.5)
    vf = v.astype(jnp.float32).reshape(b, s, N_RET_HEADS, RET_HEAD_DIM)
    qf = qf * cos + rotate_every_two(qf) * sin
    kf = kf * cos + rotate_every_two(kf) * sin
    o = retention_chunkwise(qf, kf, vf)
    mu = jnp.mean(o, axis=-1, keepdims=True)
    var = jnp.mean(jnp.square(o - mu), axis=-1, keepdims=True)
    o = ((o - mu) * lax.rsqrt(var + GN_EPS)).reshape(b, s, RET_W)
    return (jax.nn.silu(gate.astype(jnp.float32)) * o).astype(dtype)


def causal_depthwise_conv(u, w, bias):
    ch = u.shape[-1]
    y = lax.conv_general_dilated(u, w[:, None, :].astype(u.dtype), window_strides=(1,),
                                 padding=[(CONV_WIDTH - 1, 0)],
                                 dimension_numbers=('NWC', 'WIO', 'NWC'),
                                 feature_group_count=ch)
    return y + bias.astype(u.dtype)


def setup_inputs(seed: int = 0) -> dict:
    key = jax.random.key(seed)
    ks = jax.random.split(key, 13)
    f32 = jnp.float32

    def gain(k):
        return 1.0 + 0.02 * jax.random.normal(k, (DEPTH, D_MODEL), f32)

    return {
        "x": jax.random.normal(ks[0], (BATCH, SEQ, D_MODEL), f32),
        "mix_pre_norm": gain(ks[1]),
        "w_in": jax.random.normal(ks[2], (DEPTH, D_MODEL, IN_W), f32) * D_MODEL ** -0.5,
        "attn_sinks": jax.random.normal(ks[3], (DEPTH, N_ATTN_HEADS), f32),
        "w_out": jax.random.normal(ks[4], (DEPTH, MIX_W, D_MODEL), f32) * MIX_W ** -0.5,
        "mix_post_norm": gain(ks[5]),
        "ffn_pre_norm": gain(ks[6]),
        "w_up": jax.random.normal(ks[7], (DEPTH, D_MODEL, 2 * D_FF), f32) * D_MODEL ** -0.5,
        "conv_w": jax.random.normal(ks[8], (DEPTH, CONV_WIDTH, 2 * D_FF), f32) * CONV_WIDTH ** -0.5,
        "conv_b": 0.01 * jax.random.normal(ks[9], (DEPTH, 2 * D_FF), f32),
        "w_down": jax.random.normal(ks[10], (DEPTH, D_FF, D_MODEL), f32) * D_FF ** -0.5,
        "ffn_post_norm": gain(ks[11]),
    }


def reference(x, mix_pre_norm, w_in, attn_sinks, w_out, mix_post_norm,
              ffn_pre_norm, w_up, conv_w, conv_b, w_down, ffn_post_norm):
    splits = np.cumsum([ATTN_W, KV_W, KV_W, RET_W, RET_W, RET_W]).tolist()
    for l in range(DEPTH):
        h = rms_norm(x, mix_pre_norm[l])
        proj = jnp.einsum('bsd,de->bse', h, w_in[l])
        q_a, k_a, v_a, q_r, k_r, v_r, g_r = jnp.split(proj, splits, axis=-1)
        attn_out = sliding_window_sink_attention(q_a, k_a, v_a, attn_sinks[l])
        ret_out = retention_group(q_r, k_r, v_r, g_r)
        mixed = jnp.einsum('bse,ed->bsd', jnp.concatenate([attn_out, ret_out], axis=-1), w_out[l])
        x = x + rms_norm(mixed, mix_post_norm[l])
        h = rms_norm(x, ffn_pre_norm[l])
        u = causal_depthwise_conv(jnp.einsum('bsd,df->bsf', h, w_up[l]), conv_w[l], conv_b[l])
        u_gate, u_val = jnp.split(u, 2, axis=-1)
        y = jax.nn.gelu(u_gate, approximate=True) * u_val
        y = jnp.einsum('bsf,fd->bsd', y, w_down[l])
        x = x + rms_norm(y, ffn_post_norm[l])
    return x
```

```python
import functools

import numpy as np
import jax
import jax.numpy as jnp
from jax import lax
from jax.experimental import pallas as pl
from jax.experimental.pallas import tpu as pltpu

D_MODEL = 1024
HEAD_DIM = 64
ATTN_W = 512
N_ATTN_HEADS = 8
N_KV_HEADS = 2
KV_W = N_KV_HEADS * HEAD_DIM
WINDOW = 128
RET_W = 512
N_RET_HEADS = 4
RET_HEAD_DIM = 128
CHUNK = 128
IN_W = ATTN_W + 2 * KV_W + 4 * RET_W
D_FF = 2816
RMS_EPS = 1e-6
GN_EPS = 1e-6
MASK_VALUE = -1e30

LANES = 128
SUBLANES = 8
SEQ_BLOCK = 256
FF_TILE = 256
N_FF_TILES = D_FF // FF_TILE
VMEM_LIMIT_BYTES = 56 * 1024 * 1024

Q_A, K_A, V_A = 0, ATTN_W, ATTN_W + KV_W
Q_R = ATTN_W + 2 * KV_W
K_R, V_R, G_R = Q_R + RET_W, Q_R + 2 * RET_W, Q_R + 3 * RET_W


def _retention_tables(seq):
    n_chunks = seq // CHUNK
    angle = 1.0 / np.power(10000.0, np.linspace(0.0, 1.0, RET_HEAD_DIM // 2))
    angle = np.repeat(angle, 2)
    base = (np.arange(n_chunks, dtype=np.float64) * CHUNK)[:, None] * angle[None]
    intra = np.arange(CHUNK, dtype=np.float64)[:, None] * angle[None]
    gamma = 1.0 - np.power(2.0, -5.0 - np.arange(N_RET_HEADS, dtype=np.float64))
    idx = np.arange(CHUNK, dtype=np.float64)
    rel = idx[:, None] - idx[None, :]
    d_intra = np.where(rel[None] >= 0, gamma[:, None, None] ** np.maximum(rel, 0.0)[None], 0.0)
    xi = gamma[:, None] ** (idx[None, :] + 1.0)
    zeta = gamma[:, None] ** (CHUNK - 1.0 - idx[None, :])
    f32 = lambda a: jnp.asarray(a, dtype=jnp.float32)
    return dict(
        cos_base=f32(np.cos(base)), sin_base=f32(np.sin(base)),
        cos_intra=f32(np.cos(intra)), sin_intra=f32(np.sin(intra)),
        d_intra=f32(d_intra),
        xi=f32(np.broadcast_to(xi[:, :, None], (N_RET_HEADS, CHUNK, LANES))),
        zeta=f32(np.broadcast_to(zeta[:, :, None], (N_RET_HEADS, CHUNK, LANES))),
        chunk_decay=[float(g ** CHUNK) for g in gamma],
    )


def _rms(x, w):
    return x * lax.rsqrt(jnp.mean(x * x, axis=-1, keepdims=True) + RMS_EPS) * w


def _dot(a, b):
    return jnp.dot(a, b, preferred_element_type=jnp.float32)


def _dot_nt(a, b):
    return lax.dot_general(a, b, (((1,), (1,)), ((), ())), preferred_element_type=jnp.float32)


def _dot_tn(a, b):
    return lax.dot_general(a, b, (((0,), (0,)), ((), ())), preferred_element_type=jnp.float32)


def _layer_kernel(chunk_decay,
                  sinks_ref, x_ref, g1_ref, g2_ref, g3_ref, g4_ref,
                  w_in_ref, w_out_ref, w_up_ref, w_down_ref, cw_ref, cb_ref,
                  cosb_ref, sinb_ref, cosi_ref, sini_ref, dint_ref, xi_ref, zeta_ref,
                  o_ref,
                  kprev_ref, vprev_ref, state_ref, pbuf_ref, mix_ref):
    step = pl.program_id(0)
    bf16 = jnp.bfloat16
    n_sub = SEQ_BLOCK // CHUNK

    @pl.when(step == 0)
    def _init():
        kprev_ref[...] = jnp.zeros_like(kprev_ref)
        vprev_ref[...] = jnp.zeros_like(vprev_ref)
        state_ref[...] = jnp.zeros_like(state_ref)
        pbuf_ref[...] = jnp.zeros_like(pbuf_ref)

    x = x_ref[...]
    h = _rms(x, g1_ref[...]).astype(bf16)
    proj = _dot(h, w_in_ref[...])

    lane = lax.broadcasted_iota(jnp.int32, (1, LANES), 1)
    lane_lo = lane < HEAD_DIM
    lane_even = (lane % 2) == 0

    row = lax.broadcasted_iota(jnp.int32, (CHUNK, CHUNK), 0)
    col = lax.broadcasted_iota(jnp.int32, (CHUNK, CHUNK), 1)
    band_prev = col > row
    band_cur = col <= row
    scale = HEAD_DIM ** -0.5
    for b in range(n_sub):
        r0 = b * CHUNK
        k_cur = proj[r0:r0 + CHUNK, K_A:K_A + KV_W]
        v_cur = proj[r0:r0 + CHUNK, V_A:V_A + KV_W]
        if b == 0:
            k_prev, v_prev = kprev_ref[...], vprev_ref[...]
            prev_ok = jnp.logical_and(band_prev, step > 0)
        else:
            k_prev = proj[r0 - CHUNK:r0, K_A:K_A + KV_W]
            v_prev = proj[r0 - CHUNK:r0, V_A:V_A + KV_W]
            prev_ok = band_prev
        valid = jnp.concatenate([prev_ok, band_cur], axis=1)
        kk = jnp.concatenate([k_prev, k_cur], axis=0)
        vv = jnp.concatenate([v_prev, v_cur], axis=0)
        kk_sw = pltpu.roll(kk, HEAD_DIM, 1)
        vv_sw = pltpu.roll(vv, HEAD_DIM, 1)
        for hk in range(N_KV_HEADS):
            k_src_lo, k_src_hi = (kk, kk_sw) if hk == 0 else (kk_sw, kk)
            v_src_lo, v_src_hi = (vv, vv_sw) if hk == 0 else (vv_sw, vv)
            k_e = jnp.where(lane_lo, k_src_lo, 0.0).astype(bf16)
            k_o = jnp.where(lane_lo, 0.0, k_src_hi).astype(bf16)
            v_e = jnp.where(lane_lo, v_src_lo, 0.0).astype(bf16)
            v_o = jnp.where(lane_lo, 0.0, v_src_hi).astype(bf16)
            for c in range(2):
                cc = hk * 2 + c
                q = proj[r0:r0 + CHUNK, Q_A + cc * LANES:Q_A + (cc + 1) * LANES].astype(bf16)
                out = None
                for par, (k_m, v_m) in enumerate(((k_e, v_e), (k_o, v_o))):
                    sink = sinks_ref[0, 2 * cc + par]
                    s = _dot_nt(q, k_m) * scale
                    s = jnp.where(valid, s, MASK_VALUE)
                    m = jnp.maximum(jnp.max(s, axis=-1, keepdims=True), sink)
                    p = jnp.exp(s - m)
                    denom = jnp.sum(p, axis=-1, keepdims=True) + jnp.exp(sink - m)
                    p = (p / denom).astype(bf16)
                    o = _dot(p, v_m)
                    out = o if out is None else out + o
                mix_ref[r0:r0 + CHUNK, cc * LANES:(cc + 1) * LANES] = out.astype(bf16)
    kprev_ref[...] = proj[SEQ_BLOCK - CHUNK:, K_A:K_A + KV_W]
    vprev_ref[...] = proj[SEQ_BLOCK - CHUNK:, V_A:V_A + KV_W]

    k_scale = RET_HEAD_DIM ** -0.5
    cos_i, sin_i = cosi_ref[...], sini_ref[...]
    for c in range(n_sub):
        r0 = c * CHUNK
        chunk_idx = step * n_sub + c
        cos_b = cosb_ref[pl.ds(chunk_idx, 1), :]
        sin_b = sinb_ref[pl.ds(chunk_idx, 1), :]
        cos = cos_b * cos_i - sin_b * sin_i
        sin = sin_b * cos_i + cos_b * sin_i
        sin_signed = jnp.where(lane_even, -sin, sin)

        def rotate(t):
            swapped = jnp.where(lane_even, pltpu.roll(t, LANES - 1, 1), pltpu.roll(t, 1, 1))
            return t * cos + swapped * sin_signed

        for hh in range(N_RET_HEADS):
            c0 = hh * RET_HEAD_DIM
            q = rotate(proj[r0:r0 + CHUNK, Q_R + c0:Q_R + c0 + RET_HEAD_DIM])
            k = rotate(proj[r0:r0 + CHUNK, K_R + c0:K_R + c0 + RET_HEAD_DIM] * k_scale)
            v = proj[r0:r0 + CHUNK, V_R + c0:V_R + c0 + RET_HEAD_DIM].astype(bf16)
            gate = proj[r0:r0 + CHUNK, G_R + c0:G_R + c0 + RET_HEAD_DIM]
            qb = q.astype(bf16)
            inner = _dot_nt(qb, k.astype(bf16)) * dint_ref[hh]
            state = state_ref[hh]
            o = _dot(inner.astype(bf16), v) + xi_ref[hh] * _dot(qb, state.astype(bf16))
            state_ref[hh] = chunk_decay[hh] * state + _dot_tn((k * zeta_ref[hh]).astype(bf16), v)
            mu = jnp.mean(o, axis=-1, keepdims=True)
            d = o - mu
            var = jnp.mean(d * d, axis=-1, keepdims=True)
            on = d * lax.rsqrt(var + GN_EPS)
            res = gate * (1.0 / (1.0 + jnp.exp(-gate))) * on
            mix_ref[r0:r0 + CHUNK, ATTN_W + c0:ATTN_W + c0 + RET_HEAD_DIM] = res.astype(bf16)

    mixed = _dot(mix_ref[...], w_out_ref[...])
    x1 = x + _rms(mixed, g2_ref[...])

    h2 = _rms(x1, g3_ref[...]).astype(bf16)
    T = SEQ_BLOCK
    acc = jnp.zeros((T, D_MODEL), jnp.float32)
    for j in range(N_FF_TILES):
        halves = []
        for part in range(2):
            t = part * N_FF_TILES + j
            c0 = t * FF_TILE
            p = _dot(h2, w_up_ref[:, c0:c0 + FF_TILE])
            pbuf_ref[t, SUBLANES:SUBLANES + T, :] = p
            p1 = pbuf_ref[t, SUBLANES - 1:SUBLANES - 1 + T, :]
            p2 = pbuf_ref[t, SUBLANES - 2:SUBLANES - 2 + T, :]
            pbuf_ref[t, SUBLANES - 2:SUBLANES, :] = pbuf_ref[t, SUBLANES + T - 2:SUBLANES + T, :]
            u = (cw_ref[2:3, c0:c0 + FF_TILE] * p + cw_ref[1:2, c0:c0 + FF_TILE] * p1
                 + cw_ref[0:1, c0:c0 + FF_TILE] * p2 + cb_ref[:, c0:c0 + FF_TILE])
            halves.append(u)
        ug, uv = halves
        gelu = 0.5 * ug * (1.0 + jnp.tanh(0.7978845608028654 * (ug + 0.044715 * ug * ug * ug)))
        y = (gelu * uv).astype(bf16)
        acc = acc + _dot(y, w_down_ref[j * FF_TILE:(j + 1) * FF_TILE, :])
    o_ref[...] = x1 + _rms(acc, g4_ref[...])


def kernel(x, mix_pre_norm, w_in, attn_sinks, w_out, mix_post_norm, ffn_pre_norm, w_up,
           conv_w, conv_b, w_down, ffn_post_norm):
    batch, seq, d_model = x.shape
    depth = w_in.shape[0]
    assert d_model == D_MODEL and seq % SEQ_BLOCK == 0
    tabs = _retention_tables(seq)
    n_steps = seq // SEQ_BLOCK
    bf16 = jnp.bfloat16

    def resident(shape):
        nd = len(shape)
        return pl.BlockSpec(shape, lambda i, _n=nd: (0,) * _n, pipeline_mode=pl.Buffered(1))

    in_specs = [
        pl.BlockSpec(memory_space=pltpu.SMEM),
        pl.BlockSpec((SEQ_BLOCK, D_MODEL), lambda i: (i, 0)),
        resident((1, D_MODEL)), resident((1, D_MODEL)), resident((1, D_MODEL)), resident((1, D_MODEL)),
        resident((D_MODEL, IN_W)), resident((D_MODEL, D_MODEL)),
        resident((D_MODEL, 2 * D_FF)), resident((D_FF, D_MODEL)),
        resident((3, 2 * D_FF)), resident((1, 2 * D_FF)),
        resident(tabs["cos_base"].shape), resident(tabs["sin_base"].shape),
        resident((CHUNK, LANES)), resident((CHUNK, LANES)),
        resident((N_RET_HEADS, CHUNK, CHUNK)),
        resident((N_RET_HEADS, CHUNK, LANES)), resident((N_RET_HEADS, CHUNK, LANES)),
    ]
    call = pl.pallas_call(
        functools.partial(_layer_kernel, tabs["chunk_decay"]),
        grid=(n_steps,),
        in_specs=in_specs,
        out_specs=pl.BlockSpec((SEQ_BLOCK, D_MODEL), lambda i: (i, 0)),
        out_shape=jax.ShapeDtypeStruct((seq, D_MODEL), jnp.float32),
        scratch_shapes=[
            pltpu.VMEM((CHUNK, KV_W), jnp.float32),
            pltpu.VMEM((CHUNK, KV_W), jnp.float32),
            pltpu.VMEM((N_RET_HEADS, RET_HEAD_DIM, RET_HEAD_DIM), jnp.float32),
            pltpu.VMEM((2 * N_FF_TILES, SEQ_BLOCK + SUBLANES, FF_TILE), jnp.float32),
            pltpu.VMEM((SEQ_BLOCK, D_MODEL), bf16),
        ],
        compiler_params=pltpu.CompilerParams(
            dimension_semantics=("arbitrary",), vmem_limit_bytes=VMEM_LIMIT_BYTES),
        name="hybrid_layer",
    )

    outs = []
    for bi in range(batch):
        xb = x[bi]
        for l in range(depth):
            xb = call(
                attn_sinks[l].reshape(1, N_ATTN_HEADS),
                xb,
                mix_pre_norm[l].reshape(1, D_MODEL), mix_post_norm[l].reshape(1, D_MODEL),
                ffn_pre_norm[l].reshape(1, D_MODEL), ffn_post_norm[l].reshape(1, D_MODEL),
                w_in[l].astype(bf16), w_out[l].astype(bf16),
                w_up[l].astype(bf16), w_down[l].astype(bf16),
                conv_w[l], conv_b[l].reshape(1, 2 * D_FF),
                tabs["cos_base"], tabs["sin_base"], tabs["cos_intra"], tabs["sin_intra"],
                tabs["d_intra"], tabs["xi"], tabs["zeta"],
            )
        outs.append(xb)
    return outs[0][None] if batch == 1 else jnp.stack(outs, axis=0)
```

```python
import functools

import numpy as np
import jax
import jax.numpy as jnp
from jax import lax
from jax.experimental import pallas as pl
from jax.experimental.pallas import tpu as pltpu

D_MODEL = 1024
HEAD_DIM = 64
ATTN_W = 512
N_ATTN_HEADS = 8
N_KV_HEADS = 2
KV_W = N_KV_HEADS * HEAD_DIM
WINDOW = 128
RET_W = 512
N_RET_HEADS = 4
RET_HEAD_DIM = 128
CHUNK = 128
IN_W = ATTN_W + 2 * KV_W + 4 * RET_W
D_FF = 2816
RMS_EPS = 1e-6
GN_EPS = 1e-6
MASK_VALUE = -1e30

LANES = 128
SUBLANES = 8
SEQ_BLOCK = 256
FF_TILE = 256
N_FF_TILES = D_FF // FF_TILE
CONV_ROW0 = 2 * SUBLANES
VMEM_LIMIT_BYTES = 56 * 1024 * 1024

Q_A, K_A, V_A = 0, ATTN_W, ATTN_W + KV_W
Q_R = ATTN_W + 2 * KV_W
K_R, V_R, G_R = Q_R + RET_W, Q_R + 2 * RET_W, Q_R + 3 * RET_W


def _retention_tables(seq):
    n_chunks = seq // CHUNK
    angle = 1.0 / np.power(10000.0, np.linspace(0.0, 1.0, RET_HEAD_DIM // 2))
    angle = np.repeat(angle, 2)
    base = (np.arange(n_chunks, dtype=np.float64) * CHUNK)[:, None] * angle[None]
    intra = np.arange(CHUNK, dtype=np.float64)[:, None] * angle[None]
    gamma = 1.0 - np.power(2.0, -5.0 - np.arange(N_RET_HEADS, dtype=np.float64))
    idx = np.arange(CHUNK, dtype=np.float64)
    rel = idx[:, None] - idx[None, :]
    d_intra = np.where(rel[None] >= 0, gamma[:, None, None] ** np.maximum(rel, 0.0)[None], 0.0)
    xi = gamma[:, None] ** (idx[None, :] + 1.0)
    zeta = gamma[:, None] ** (CHUNK - 1.0 - idx[None, :])
    f32 = lambda a: jnp.asarray(a, dtype=jnp.float32)
    return dict(
        cos_base=f32(np.cos(base)), sin_base=f32(np.sin(base)),
        cos_intra=f32(np.cos(intra)), sin_intra=f32(np.sin(intra)),
        d_intra=f32(d_intra),
        xi=f32(np.broadcast_to(xi[:, :, None], (N_RET_HEADS, CHUNK, LANES))),
        zeta=f32(np.broadcast_to(zeta[:, :, None], (N_RET_HEADS, CHUNK, LANES))),
        chunk_decay=[float(g ** CHUNK) for g in gamma],
    )


def _rms(x, w):
    return x * lax.rsqrt(jnp.mean(x * x, axis=-1, keepdims=True) + RMS_EPS) * w


def _dot(a, b):
    return jnp.dot(a, b, preferred_element_type=jnp.float32)


def _dot_nt(a, b):
    return lax.dot_general(a, b, (((1,), (1,)), ((), ())), preferred_element_type=jnp.float32)


def _dot_tn(a, b):
    return lax.dot_general(a, b, (((0,), (0,)), ((), ())), preferred_element_type=jnp.float32)


def _layer_kernel(chunk_decay,
                  sinks_ref, x_ref, g1_ref, g2_ref, g3_ref, g4_ref,
                  w_in_ref, w_out_ref, w_up_ref, w_down_ref, cw_ref, cb_ref,
                  cosb_ref, sinb_ref, cosi_ref, sini_ref, dint_ref, xi_ref, zeta_ref,
                  o_ref,
                  kprev_ref, vprev_ref, state_ref, pbuf_ref, mix_ref, y_ref):
    step = pl.program_id(0)
    bf16 = jnp.bfloat16
    n_sub = SEQ_BLOCK // CHUNK

    @pl.when(step == 0)
    def _init():
        kprev_ref[...] = jnp.zeros_like(kprev_ref)
        vprev_ref[...] = jnp.zeros_like(vprev_ref)
        state_ref[...] = jnp.zeros_like(state_ref)
        pbuf_ref[...] = jnp.zeros_like(pbuf_ref)

    x = x_ref[...]
    h = _rms(x, g1_ref[...]).astype(bf16)
    proj = _dot(h, w_in_ref[...])

    lane = lax.broadcasted_iota(jnp.int32, (1, LANES), 1)
    lane_lo = lane < HEAD_DIM
    lane_even = (lane % 2) == 0

    row = lax.broadcasted_iota(jnp.int32, (CHUNK, CHUNK), 0)
    col = lax.broadcasted_iota(jnp.int32, (CHUNK, CHUNK), 1)
    band_prev = col > row
    band_cur = col <= row
    scale = HEAD_DIM ** -0.5
    for b in range(n_sub):
        r0 = b * CHUNK
        k_cur = proj[r0:r0 + CHUNK, K_A:K_A + KV_W]
        v_cur = proj[r0:r0 + CHUNK, V_A:V_A + KV_W]
        if b == 0:
            k_prev, v_prev = kprev_ref[...], vprev_ref[...]
            prev_ok = jnp.logical_and(band_prev, step > 0)
        else:
            k_prev = proj[r0 - CHUNK:r0, K_A:K_A + KV_W]
            v_prev = proj[r0 - CHUNK:r0, V_A:V_A + KV_W]
            prev_ok = band_prev
        valid = jnp.concatenate([prev_ok, band_cur], axis=1)
        kk = jnp.concatenate([k_prev, k_cur], axis=0)
        vv = jnp.concatenate([v_prev, v_cur], axis=0)
        kk_sw = pltpu.roll(kk, HEAD_DIM, 1)
        vv_sw = pltpu.roll(vv, HEAD_DIM, 1)
        for hk in range(N_KV_HEADS):
            k_src_lo, k_src_hi = (kk, kk_sw) if hk == 0 else (kk_sw, kk)
            v_src_lo, v_src_hi = (vv, vv_sw) if hk == 0 else (vv_sw, vv)
            k_e = jnp.where(lane_lo, k_src_lo, 0.0).astype(bf16)
            k_o = jnp.where(lane_lo, 0.0, k_src_hi).astype(bf16)
            v_e = jnp.where(lane_lo, v_src_lo, 0.0).astype(bf16)
            v_o = jnp.where(lane_lo, 0.0, v_src_hi).astype(bf16)
            for c in range(2):
                cc = hk * 2 + c
                q = proj[r0:r0 + CHUNK, Q_A + cc * LANES:Q_A + (cc + 1) * LANES].astype(bf16)
                out = None
                for par, (k_m, v_m) in enumerate(((k_e, v_e), (k_o, v_o))):
                    sink = sinks_ref[0, 2 * cc + par]
                    s = _dot_nt(q, k_m) * scale
                    s = jnp.where(valid, s, MASK_VALUE)
                    m = jnp.maximum(jnp.max(s, axis=-1, keepdims=True), sink)
                    p = jnp.exp(s - m)
                    denom = jnp.sum(p, axis=-1, keepdims=True) + jnp.exp(sink - m)
                    p = (p / denom).astype(bf16)
                    o = _dot(p, v_m)
                    out = o if out is None else out + o
                mix_ref[r0:r0 + CHUNK, cc * LANES:(cc + 1) * LANES] = out.astype(bf16)
    kprev_ref[...] = proj[SEQ_BLOCK - CHUNK:, K_A:K_A + KV_W]
    vprev_ref[...] = proj[SEQ_BLOCK - CHUNK:, V_A:V_A + KV_W]

    k_scale = RET_HEAD_DIM ** -0.5
    cos_i, sin_i = cosi_ref[...], sini_ref[...]
    for c in range(n_sub):
        r0 = c * CHUNK
        chunk_idx = step * n_sub + c
        cos_b = cosb_ref[pl.ds(chunk_idx, 1), :]
        sin_b = sinb_ref[pl.ds(chunk_idx, 1), :]
        cos = cos_b * cos_i - sin_b * sin_i
        sin = sin_b * cos_i + cos_b * sin_i
        sin_signed = jnp.where(lane_even, -sin, sin)

        def rotate(t):
            swapped = jnp.where(lane_even, pltpu.roll(t, LANES - 1, 1), pltpu.roll(t, 1, 1))
            return t * cos + swapped * sin_signed

        for hh in range(N_RET_HEADS):
            c0 = hh * RET_HEAD_DIM
            q = rotate(proj[r0:r0 + CHUNK, Q_R + c0:Q_R + c0 + RET_HEAD_DIM])
            k = rotate(proj[r0:r0 + CHUNK, K_R + c0:K_R + c0 + RET_HEAD_DIM] * k_scale)
            v = proj[r0:r0 + CHUNK, V_R + c0:V_R + c0 + RET_HEAD_DIM].astype(bf16)
            gate = proj[r0:r0 + CHUNK, G_R + c0:G_R + c0 + RET_HEAD_DIM]
            qb = q.astype(bf16)
            inner = _dot_nt(qb, k.astype(bf16)) * dint_ref[hh]
            state = state_ref[hh]
            o = _dot(inner.astype(bf16), v) + xi_ref[hh] * _dot(qb, state.astype(bf16))
            state_ref[hh] = chunk_decay[hh] * state + _dot_tn((k * zeta_ref[hh]).astype(bf16), v)
            mu = jnp.mean(o, axis=-1, keepdims=True)
            d = o - mu
            var = jnp.mean(d * d, axis=-1, keepdims=True)
            on = d * lax.rsqrt(var + GN_EPS)
            res = gate * (1.0 / (1.0 + jnp.exp(-gate))) * on
            mix_ref[r0:r0 + CHUNK, ATTN_W + c0:ATTN_W + c0 + RET_HEAD_DIM] = res.astype(bf16)

    mixed = _dot(mix_ref[...], w_out_ref[...])
    x1 = x + _rms(mixed, g2_ref[...])

    h2 = _rms(x1, g3_ref[...]).astype(bf16)
    T = SEQ_BLOCK
    def conv(p, slab, c0):
        pbuf_ref[slab, pl.ds(CONV_ROW0, T, stride=2), :] = p
        p1 = pbuf_ref[slab, pl.ds(CONV_ROW0 - 2, T, stride=2), :]
        p2 = pbuf_ref[slab, pl.ds(CONV_ROW0 - 4, T, stride=2), :]
        pbuf_ref[slab, CONV_ROW0 - 4:CONV_ROW0, :] = pbuf_ref[slab, CONV_ROW0 + 2 * T - 4:CONV_ROW0 + 2 * T, :]
        return (cw_ref[2:3, c0:c0 + LANES] * p + cw_ref[1:2, c0:c0 + LANES] * p1
                + cw_ref[0:1, c0:c0 + LANES] * p2 + cb_ref[:, c0:c0 + LANES])

    for j in range(N_FF_TILES):
        c_gate, c_val = j * FF_TILE, D_FF + j * FF_TILE
        p_gate = _dot(h2, w_up_ref[:, c_gate:c_gate + FF_TILE])
        p_val = _dot(h2, w_up_ref[:, c_val:c_val + FF_TILE])
        for half in range(FF_TILE // LANES):
            l0 = half * LANES
            ug = conv(p_gate[:, l0:l0 + LANES], (c_gate + l0) // LANES, c_gate + l0)
            uv = conv(p_val[:, l0:l0 + LANES], (c_val + l0) // LANES, c_val + l0)
            gelu = 0.5 * ug * (1.0 + jnp.tanh(0.7978845608028654 * (ug + 0.044715 * ug * ug * ug)))
            y_ref[:, c_gate + l0:c_gate + l0 + LANES] = (gelu * uv).astype(bf16)
    down = _dot(y_ref[...], w_down_ref[...])
    o_ref[...] = x1 + _rms(down, g4_ref[...])


def kernel(x, mix_pre_norm, w_in, attn_sinks, w_out, mix_post_norm, ffn_pre_norm, w_up,
           conv_w, conv_b, w_down, ffn_post_norm):
    batch, seq, d_model = x.shape
    depth = w_in.shape[0]
    assert d_model == D_MODEL and seq % SEQ_BLOCK == 0
    tabs = _retention_tables(seq)
    n_steps = seq // SEQ_BLOCK
    bf16 = jnp.bfloat16

    def resident(shape):
        nd = len(shape)
        return pl.BlockSpec(shape, lambda i, _n=nd: (0,) * _n, pipeline_mode=pl.Buffered(1))

    in_specs = [
        pl.BlockSpec(memory_space=pltpu.SMEM),
        pl.BlockSpec((SEQ_BLOCK, D_MODEL), lambda i: (i, 0)),
        resident((1, D_MODEL)), resident((1, D_MODEL)), resident((1, D_MODEL)), resident((1, D_MODEL)),
        resident((D_MODEL, IN_W)), resident((D_MODEL, D_MODEL)),
        resident((D_MODEL, 2 * D_FF)), resident((D_FF, D_MODEL)),
        resident((3, 2 * D_FF)), resident((1, 2 * D_FF)),
        resident(tabs["cos_base"].shape), resident(tabs["sin_base"].shape),
        resident((CHUNK, LANES)), resident((CHUNK, LANES)),
        resident((N_RET_HEADS, CHUNK, CHUNK)),
        resident((N_RET_HEADS, CHUNK, LANES)), resident((N_RET_HEADS, CHUNK, LANES)),
    ]
    call = pl.pallas_call(
        functools.partial(_layer_kernel, tabs["chunk_decay"]),
        grid=(n_steps,),
        in_specs=in_specs,
        out_specs=pl.BlockSpec((SEQ_BLOCK, D_MODEL), lambda i: (i, 0)),
        out_shape=jax.ShapeDtypeStruct((seq, D_MODEL), jnp.float32),
        scratch_shapes=[
            pltpu.VMEM((CHUNK, KV_W), jnp.float32),
            pltpu.VMEM((CHUNK, KV_W), jnp.float32),
            pltpu.VMEM((N_RET_HEADS, RET_HEAD_DIM, RET_HEAD_DIM), jnp.float32),
            pltpu.VMEM((2 * D_FF // LANES, CONV_ROW0 + 2 * SEQ_BLOCK, LANES), jnp.float32),
            pltpu.VMEM((SEQ_BLOCK, D_MODEL), bf16),
            pltpu.VMEM((SEQ_BLOCK, D_FF), bf16),
        ],
        compiler_params=pltpu.CompilerParams(
            dimension_semantics=("arbitrary",), vmem_limit_bytes=VMEM_LIMIT_BYTES),
        name="hybrid_layer",
    )

    outs = []
    for bi in range(batch):
        xb = x[bi]
        for l in range(depth):
            xb = call(
                attn_sinks[l].reshape(1, N_ATTN_HEADS),
                xb,
                mix_pre_norm[l].reshape(1, D_MODEL), mix_post_norm[l].reshape(1, D_MODEL),
                ffn_pre_norm[l].reshape(1, D_MODEL), ffn_post_norm[l].reshape(1, D_MODEL),
                w_in[l].astype(bf16), w_out[l].astype(bf16),
                w_up[l].astype(bf16), w_down[l].astype(bf16),
                conv_w[l], conv_b[l].reshape(1, 2 * D_FF),
                tabs["cos_base"], tabs["sin_base"], tabs["cos_intra"], tabs["sin_intra"],
                tabs["d_intra"], tabs["xi"], tabs["zeta"],
            )
        outs.append(xb)
    return outs[0][None] if batch == 1 else jnp.stack(outs, axis=0)
```

```python
import functools

import numpy as np
import jax
import jax.numpy as jnp
from jax import lax
from jax.experimental import pallas as pl
from jax.experimental.pallas import tpu as pltpu

D_MODEL = 1024
HEAD_DIM = 64
ATTN_W = 512
N_ATTN_HEADS = 8
N_KV_HEADS = 2
Q_PER_KV = N_ATTN_HEADS // N_KV_HEADS
KV_W = N_KV_HEADS * HEAD_DIM
WINDOW = 128
RET_W = 512
N_RET_HEADS = 4
RET_HEAD_DIM = 128
D_FF = 2816
RMS_EPS = 1e-6
GN_EPS = 1e-6
MASK_VALUE = -1e30

LANES = 128
SUBLANES = 8
SEQ_BLOCK = 256
FF_TILE = 256
N_FF_TILES = D_FF // FF_TILE
CONV_ROW0 = 2 * SUBLANES
VMEM_LIMIT_BYTES = 56 * 1024 * 1024

T_QA, T_VA, T_VR, T_GR = 0, ATTN_W, ATTN_W + KV_W, ATTN_W + KV_W + RET_W
T_ROWS = ATTN_W + KV_W + 2 * RET_W
S_KA, S_QR, S_KR = 0, KV_W, KV_W + RET_W
S_COLS = KV_W + 2 * RET_W


def _retention_tables(seq):
    c = SEQ_BLOCK
    n_steps = seq // c
    angle = 1.0 / np.power(10000.0, np.linspace(0.0, 1.0, RET_HEAD_DIM // 2))
    angle = np.repeat(angle, 2)
    base = (np.arange(n_steps, dtype=np.float64) * c)[:, None] * angle[None]
    intra = np.arange(c, dtype=np.float64)[:, None] * angle[None]
    gamma = 1.0 - np.power(2.0, -5.0 - np.arange(N_RET_HEADS, dtype=np.float64))
    k_scale = RET_HEAD_DIM ** -0.5
    idx = np.arange(c, dtype=np.float64)
    rel = idx[None, :] - idx[:, None]
    decay_t = np.where(rel[None] >= 0, gamma[:, None, None] ** np.maximum(rel, 0.0)[None], 0.0)
    xi = gamma[:, None] ** (idx[None, :] + 1.0)
    zeta = gamma[:, None] ** (c - 1.0 - idx[None, :])
    f32 = lambda a: jnp.asarray(a, dtype=jnp.float32)
    return dict(
        cos_base=f32(np.cos(base)), sin_base=f32(np.sin(base)),
        cos_intra=f32(np.cos(intra)), sin_intra=f32(np.sin(intra)),
        decay_t=f32(k_scale * decay_t),
        xi=f32(np.broadcast_to(xi[:, :, None], (N_RET_HEADS, c, LANES))),
        zeta=f32(np.broadcast_to(k_scale * zeta[:, :, None], (N_RET_HEADS, c, LANES))),
        chunk_decay=[float(g ** c) for g in gamma],
    )


def _rms(x, w):
    return x * lax.rsqrt(jnp.mean(x * x, axis=-1, keepdims=True) + RMS_EPS) * w


def _dot(a, b):
    return jnp.dot(a, b, preferred_element_type=jnp.float32)


def _dot_nt(a, b):
    return lax.dot_general(a, b, (((1,), (1,)), ((), ())), preferred_element_type=jnp.float32)


def _dot_tn(a, b):
    return lax.dot_general(a, b, (((0,), (0,)), ((), ())), preferred_element_type=jnp.float32)


def _layer_kernel(chunk_decay,
                  x_ref, g1_ref, g2_ref, g3_ref, g4_ref, sink_ref,
                  w_t_ref, w_s_ref, w_out_ref, w_up_ref, w_down_ref, cw_ref, cb_ref,
                  cosb_ref, sinb_ref, cosi_ref, sini_ref, decay_ref, xi_ref, zeta_ref,
                  o_ref,
                  kprev_ref, vtprev_ref, state_ref, pbuf_ref, mixt_ref, y_ref):
    step = pl.program_id(0)
    bf16 = jnp.bfloat16
    T = SEQ_BLOCK
    W = WINDOW

    @pl.when(step == 0)
    def _init():
        kprev_ref[...] = jnp.zeros_like(kprev_ref)
        vtprev_ref[...] = jnp.zeros_like(vtprev_ref)
        state_ref[...] = jnp.zeros_like(state_ref)
        pbuf_ref[...] = jnp.zeros_like(pbuf_ref)

    x = x_ref[...]
    h = _rms(x, g1_ref[...]).astype(bf16)
    proj_t = _dot_nt(w_t_ref[...], h)
    proj_s = _dot(h, w_s_ref[...])

    key_idx = lax.broadcasted_iota(jnp.int32, (W, Q_PER_KV * W), 0)
    qry_idx = lax.broadcasted_iota(jnp.int32, (W, Q_PER_KV * W), 1) & (W - 1)
    from_prev = key_idx > qry_idx
    first_block_penalty = jnp.where(step == 0, MASK_VALUE, 0.0)
    k_a = proj_s[:, S_KA:S_KA + KV_W]
    vt_a = proj_t[T_VA:T_VA + KV_W, :]
    zeros_half = jnp.zeros((HEAD_DIM, Q_PER_KV * W), bf16)
    for b in range(T // W):
        t0 = b * W
        if b == 0:
            kk = jnp.concatenate([kprev_ref[...], k_a[:W]], axis=0)
            vvt = jnp.concatenate([vtprev_ref[...], vt_a[:, :W]], axis=1)
        else:
            kk = k_a[t0 - W:t0 + W]
            vvt = vt_a[:, t0 - W:t0 + W]
        kk = kk.astype(bf16)
        vvt = vvt.astype(bf16)
        for hk in range(N_KV_HEADS):
            h0 = hk * Q_PER_KV
            qt = jnp.concatenate(
                [proj_t[T_QA + (h0 + g) * HEAD_DIM:T_QA + (h0 + g + 1) * HEAD_DIM, t0:t0 + W]
                 for g in range(Q_PER_KV)], axis=1).astype(bf16)
            rhs = jnp.concatenate([qt, zeros_half] if hk == 0 else [zeros_half, qt], axis=0)
            st = _dot(kk, rhs)
            s_prev = st[:W] + first_block_penalty if b == 0 else st[:W]
            s = jnp.where(from_prev, s_prev, st[W:])
            sink = sink_ref[hk:hk + 1, :]
            m = jnp.maximum(jnp.max(s, axis=0, keepdims=True), sink)
            p = jnp.exp(s - m)
            denom = jnp.sum(p, axis=0, keepdims=True) + jnp.exp(sink - m)
            pt = jnp.concatenate([jnp.where(from_prev, p, 0.0).astype(bf16),
                                  jnp.where(from_prev, 0.0, p).astype(bf16)], axis=0)
            out_t = _dot(vvt[hk * HEAD_DIM:(hk + 1) * HEAD_DIM, :], pt) * (1.0 / denom)
            for g in range(Q_PER_KV):
                r0 = (h0 + g) * HEAD_DIM
                mixt_ref[r0:r0 + HEAD_DIM, t0:t0 + W] = out_t[:, g * W:(g + 1) * W].astype(bf16)
    kprev_ref[...] = k_a[T - W:]
    vtprev_ref[...] = vt_a[:, T - W:]

    lane = lax.broadcasted_iota(jnp.int32, (1, LANES), 1)
    lane_even = (lane & 1) == 0
    cos_b = cosb_ref[pl.ds(step, 1), :]
    sin_b = sinb_ref[pl.ds(step, 1), :]
    cos_i, sin_i = cosi_ref[...], sini_ref[...]
    cos = cos_b * cos_i - sin_b * sin_i
    sin = sin_b * cos_i + cos_b * sin_i
    sin_signed = jnp.where(lane_even, -sin, sin)

    def rotate(t):
        swapped = jnp.where(lane_even, pltpu.roll(t, LANES - 1, 1), pltpu.roll(t, 1, 1))
        return t * cos + swapped * sin_signed

    for hh in range(N_RET_HEADS):
        c0 = hh * RET_HEAD_DIM
        q = rotate(proj_s[:, S_QR + c0:S_QR + c0 + RET_HEAD_DIM])
        k = rotate(proj_s[:, S_KR + c0:S_KR + c0 + RET_HEAD_DIM])
        vt = proj_t[T_VR + c0:T_VR + c0 + RET_HEAD_DIM, :].astype(bf16)
        gate_t = proj_t[T_GR + c0:T_GR + c0 + RET_HEAD_DIM, :]
        inner_t = _dot_nt(k.astype(bf16), q.astype(bf16)) * decay_ref[hh]
        state_t = state_ref[hh]
        o_t = (_dot(vt, inner_t.astype(bf16))
               + _dot_nt(state_t.astype(bf16), (q * xi_ref[hh]).astype(bf16)))
        state_ref[hh] = chunk_decay[hh] * state_t + _dot(vt, (k * zeta_ref[hh]).astype(bf16))
        mu = jnp.mean(o_t, axis=0, keepdims=True)
        d = o_t - mu
        var = jnp.mean(d * d, axis=0, keepdims=True)
        res = d * lax.rsqrt(var + GN_EPS) * (gate_t * (1.0 / (1.0 + jnp.exp(-gate_t))))
        mixt_ref[ATTN_W + c0:ATTN_W + c0 + RET_HEAD_DIM, :] = res.astype(bf16)

    mixed = _dot_tn(mixt_ref[...], w_out_ref[...])
    x1 = x + _rms(mixed, g2_ref[...])

    h2 = _rms(x1, g3_ref[...]).astype(bf16)

    def conv(p, slab, c0):
        pbuf_ref[slab, pl.ds(CONV_ROW0, T, stride=2), :] = p
        p1 = pbuf_ref[slab, pl.ds(CONV_ROW0 - 2, T, stride=2), :]
        p2 = pbuf_ref[slab, pl.ds(CONV_ROW0 - 4, T, stride=2), :]
        pbuf_ref[slab, CONV_ROW0 - 4:CONV_ROW0, :] = pbuf_ref[slab, CONV_ROW0 + 2 * T - 4:CONV_ROW0 + 2 * T, :]
        return (cw_ref[2:3, c0:c0 + LANES] * p + cw_ref[1:2, c0:c0 + LANES] * p1
                + cw_ref[0:1, c0:c0 + LANES] * p2 + cb_ref[:, c0:c0 + LANES])

    for j in range(N_FF_TILES):
        c_gate, c_val = j * FF_TILE, D_FF + j * FF_TILE
        p_gate = _dot(h2, w_up_ref[:, c_gate:c_gate + FF_TILE])
        p_val = _dot(h2, w_up_ref[:, c_val:c_val + FF_TILE])
        for half in range(FF_TILE // LANES):
            l0 = half * LANES
            ug = conv(p_gate[:, l0:l0 + LANES], (c_gate + l0) // LANES, c_gate + l0)
            uv = conv(p_val[:, l0:l0 + LANES], (c_val + l0) // LANES, c_val + l0)
            gelu = 0.5 * ug * (1.0 + jnp.tanh(0.7978845608028654 * (ug + 0.044715 * ug * ug * ug)))
            y_ref[:, c_gate + l0:c_gate + l0 + LANES] = (gelu * uv).astype(bf16)
    down = _dot(y_ref[...], w_down_ref[...])
    o_ref[...] = x1 + _rms(down, g4_ref[...])


def kernel(x, mix_pre_norm, w_in, attn_sinks, w_out, mix_post_norm, ffn_pre_norm, w_up,
           conv_w, conv_b, w_down, ffn_post_norm):
    batch, seq, d_model = x.shape
    depth = w_in.shape[0]
    assert d_model == D_MODEL and seq % SEQ_BLOCK == 0
    tabs = _retention_tables(seq)
    n_steps = seq // SEQ_BLOCK
    bf16 = jnp.bfloat16

    def resident(shape):
        nd = len(shape)
        return pl.BlockSpec(shape, lambda i, _n=nd: (0,) * _n, pipeline_mode=pl.Buffered(1))

    in_specs = [
        pl.BlockSpec((SEQ_BLOCK, D_MODEL), lambda i: (i, 0)),
        resident((1, D_MODEL)), resident((1, D_MODEL)), resident((1, D_MODEL)), resident((1, D_MODEL)),
        resident((N_KV_HEADS, Q_PER_KV * WINDOW)),
        resident((T_ROWS, D_MODEL)), resident((D_MODEL, S_COLS)),
        resident((D_MODEL, D_MODEL)),
        resident((D_MODEL, 2 * D_FF)), resident((D_FF, D_MODEL)),
        resident((3, 2 * D_FF)), resident((1, 2 * D_FF)),
        resident((n_steps, LANES)), resident((n_steps, LANES)),
        resident((SEQ_BLOCK, LANES)), resident((SEQ_BLOCK, LANES)),
        resident((N_RET_HEADS, SEQ_BLOCK, SEQ_BLOCK)),
        resident((N_RET_HEADS, SEQ_BLOCK, LANES)), resident((N_RET_HEADS, SEQ_BLOCK, LANES)),
    ]
    call = pl.pallas_call(
        functools.partial(_layer_kernel, tabs["chunk_decay"]),
        grid=(n_steps,),
        in_specs=in_specs,
        out_specs=pl.BlockSpec((SEQ_BLOCK, D_MODEL), lambda i: (i, 0)),
        out_shape=jax.ShapeDtypeStruct((seq, D_MODEL), jnp.float32),
        scratch_shapes=[
            pltpu.VMEM((WINDOW, KV_W), jnp.float32),
            pltpu.VMEM((KV_W, WINDOW), jnp.float32),
            pltpu.VMEM((N_RET_HEADS, RET_HEAD_DIM, RET_HEAD_DIM), jnp.float32),
            pltpu.VMEM((2 * D_FF // LANES, CONV_ROW0 + 2 * SEQ_BLOCK, LANES), jnp.float32),
            pltpu.VMEM((D_MODEL, SEQ_BLOCK), bf16),
            pltpu.VMEM((SEQ_BLOCK, D_FF), bf16),
        ],
        compiler_params=pltpu.CompilerParams(
            dimension_semantics=("arbitrary",), vmem_limit_bytes=VMEM_LIMIT_BYTES),
        name="hybrid_layer",
    )

    outs = []
    for bi in range(batch):
        xb = x[bi]
        for l in range(depth):
            wl = w_in[l]
            col = lambda a, n: wl[:, a:a + n]
            q_a, k_a, v_a = col(0, ATTN_W), col(ATTN_W, KV_W), col(ATTN_W + KV_W, KV_W)
            r0 = ATTN_W + 2 * KV_W
            q_r, k_r, v_r, g_r = (col(r0 + i * RET_W, RET_W) for i in range(4))
            w_t = jnp.concatenate([q_a * (HEAD_DIM ** -0.5), v_a, v_r, g_r], axis=1).T.astype(bf16)
            w_s = jnp.concatenate([k_a, q_r, k_r], axis=1).astype(bf16)
            sink_rows = jnp.repeat(attn_sinks[l].reshape(N_KV_HEADS, Q_PER_KV), WINDOW, axis=1)
            xb = call(
                xb,
                mix_pre_norm[l].reshape(1, D_MODEL), mix_post_norm[l].reshape(1, D_MODEL),
                ffn_pre_norm[l].reshape(1, D_MODEL), ffn_post_norm[l].reshape(1, D_MODEL),
                sink_rows,
                w_t, w_s, w_out[l].astype(bf16),
                w_up[l].astype(bf16), w_down[l].astype(bf16),
                conv_w[l], conv_b[l].reshape(1, 2 * D_FF),
                tabs["cos_base"], tabs["sin_base"], tabs["cos_intra"], tabs["sin_intra"],
                tabs["decay_t"], tabs["xi"], tabs["zeta"],
            )
        outs.append(xb)
    return outs[0][None] if batch == 1 else jnp.stack(outs, axis=0)
```

```python
import functools

import numpy as np
import jax
import jax.numpy as jnp
from jax import lax
from jax.experimental import pallas as pl
from jax.experimental.pallas import tpu as pltpu

D_MODEL = 1024
HEAD_DIM = 64
ATTN_W = 512
N_ATTN_HEADS = 8
N_KV_HEADS = 2
Q_PER_KV = N_ATTN_HEADS // N_KV_HEADS
KV_W = N_KV_HEADS * HEAD_DIM
WINDOW = 128
RET_W = 512
N_RET_HEADS = 4
RET_HEAD_DIM = 128
D_FF = 2816
RMS_EPS = 1e-6
GN_EPS = 1e-6
MASK_VALUE = -1e30

LANES = 128
SUBLANES = 8
MXU_TILE = 256
SEQ_BLOCK = 256
FF_TILE = MXU_TILE
N_FF_TILES = D_FF // FF_TILE
CONV_ROW0 = 2 * SUBLANES
VMEM_LIMIT_BYTES = 56 * 1024 * 1024

T_QA, T_VA, T_VR, T_GR = 0, ATTN_W, ATTN_W + KV_W, ATTN_W + KV_W + RET_W
T_ROWS = ATTN_W + KV_W + 2 * RET_W
S_KA, S_QR, S_KR = 0, KV_W, KV_W + RET_W
S_COLS = KV_W + 2 * RET_W


def _retention_tables(seq):
    c = SEQ_BLOCK
    n_steps = seq // c
    angle = 1.0 / np.power(10000.0, np.linspace(0.0, 1.0, RET_HEAD_DIM // 2))
    angle = np.repeat(angle, 2)
    base = (np.arange(n_steps, dtype=np.float64) * c)[:, None] * angle[None]
    intra = np.arange(c, dtype=np.float64)[:, None] * angle[None]
    gamma = 1.0 - np.power(2.0, -5.0 - np.arange(N_RET_HEADS, dtype=np.float64))
    k_scale = RET_HEAD_DIM ** -0.5
    idx = np.arange(c, dtype=np.float64)
    rel = idx[None, :] - idx[:, None]
    decay_t = np.where(rel[None] >= 0, gamma[:, None, None] ** np.maximum(rel, 0.0)[None], 0.0)
    xi = gamma[:, None] ** (idx[None, :] + 1.0)
    zeta = gamma[:, None] ** (c - 1.0 - idx[None, :])
    f32 = lambda a: jnp.asarray(a, dtype=jnp.float32)
    return dict(
        cos_base=f32(np.cos(base)), sin_base=f32(np.sin(base)),
        cos_intra=f32(np.cos(intra)), sin_intra=f32(np.sin(intra)),
        decay_t=f32(k_scale * decay_t),
        xi=f32(np.broadcast_to(xi[:, :, None], (N_RET_HEADS, c, LANES))),
        zeta=f32(np.broadcast_to(k_scale * zeta[:, :, None], (N_RET_HEADS, c, LANES))),
        chunk_decay=[float(g ** c) for g in gamma],
    )


def _rms(x, w):
    return x * lax.rsqrt(jnp.mean(x * x, axis=-1, keepdims=True) + RMS_EPS) * w


def _dot(a, b):
    return jnp.dot(a, b, preferred_element_type=jnp.float32)


def _dot_nt(a, b):
    return lax.dot_general(a, b, (((1,), (1,)), ((), ())), preferred_element_type=jnp.float32)


def _dot_tn(a, b):
    return lax.dot_general(a, b, (((0,), (0,)), ((), ())), preferred_element_type=jnp.float32)


def _interleave(primary, filler):
    order, used = [], 0
    for i, unit in enumerate(primary):
        order.append(unit)
        upto = (i + 1) * len(filler) // len(primary)
        order.extend(filler[used:upto])
        used = upto
    return order


def _layer_kernel(chunk_decay, n_blocks,
                  x_ref, g1_ref, g2_ref, g3_ref, g4_ref, sink_ref,
                  w_t_ref, w_s_ref, w_out_ref, w_up_ref, w_down_ref, cw_ref, cb_ref,
                  cosb_ref, sinb_ref, cosi_ref, sini_ref, decay_ref, xi_ref, zeta_ref,
                  o_ref,
                  kprev_ref, vtprev_ref, state_ref, pbuf_ref, mixt_ref, y_ref, x1_ref):
    step = pl.program_id(0)
    bf16 = jnp.bfloat16
    T = SEQ_BLOCK
    W = WINDOW

    @pl.when(step == 0)
    def _init():
        kprev_ref[...] = jnp.zeros_like(kprev_ref)
        vtprev_ref[...] = jnp.zeros_like(vtprev_ref)
        state_ref[...] = jnp.zeros_like(state_ref)
        pbuf_ref[...] = jnp.zeros_like(pbuf_ref)
        x1_ref[...] = jnp.zeros_like(x1_ref)

    blk = jnp.minimum(step, n_blocks - 1)

    h2 = _rms(x1_ref[...], g3_ref[...]).astype(bf16)
    x = x_ref[...]
    h = _rms(x, g1_ref[...]).astype(bf16)
    proj_t = _dot_nt(w_t_ref[...], h)
    proj_s = _dot(h, w_s_ref[...])

    def conv(p, slab, c0):
        pbuf_ref[slab, pl.ds(CONV_ROW0, T, stride=2), :] = p
        p1 = pbuf_ref[slab, pl.ds(CONV_ROW0 - 2, T, stride=2), :]
        p2 = pbuf_ref[slab, pl.ds(CONV_ROW0 - 4, T, stride=2), :]
        pbuf_ref[slab, CONV_ROW0 - 4:CONV_ROW0, :] = pbuf_ref[slab, CONV_ROW0 + 2 * T - 4:CONV_ROW0 + 2 * T, :]
        return (cw_ref[2:3, c0:c0 + LANES] * p + cw_ref[1:2, c0:c0 + LANES] * p1
                + cw_ref[0:1, c0:c0 + LANES] * p2 + cb_ref[:, c0:c0 + LANES])

    def ffn_up_unit(j):
        c_gate, c_val = j * FF_TILE, D_FF + j * FF_TILE
        p_gate = _dot(h2, w_up_ref[:, c_gate:c_gate + FF_TILE])
        p_val = _dot(h2, w_up_ref[:, c_val:c_val + FF_TILE])
        for half in range(FF_TILE // LANES):
            l0 = half * LANES
            ug = conv(p_gate[:, l0:l0 + LANES], (c_gate + l0) // LANES, c_gate + l0)
            uv = conv(p_val[:, l0:l0 + LANES], (c_val + l0) // LANES, c_val + l0)
            gelu = 0.5 * ug * (1.0 + jnp.tanh(0.7978845608028654 * (ug + 0.044715 * ug * ug * ug)))
            y_ref[:, c_gate + l0:c_gate + l0 + LANES] = (gelu * uv).astype(bf16)

    def ffn_down_unit(n):
        c0 = n * MXU_TILE
        o_ref[:, c0:c0 + MXU_TILE] = _dot(y_ref[...], w_down_ref[:, c0:c0 + MXU_TILE])

    for j in range(N_FF_TILES):
        ffn_up_unit(j)

    key_idx = lax.broadcasted_iota(jnp.int32, (W, Q_PER_KV * W), 0)
    qry_idx = lax.broadcasted_iota(jnp.int32, (W, Q_PER_KV * W), 1) & (W - 1)
    from_prev = key_idx > qry_idx
    first_block_penalty = jnp.where(step == 0, MASK_VALUE, 0.0)
    k_a = proj_s[:, S_KA:S_KA + KV_W]
    vt_a = proj_t[T_VA:T_VA + KV_W, :]
    zeros_half = jnp.zeros((HEAD_DIM, Q_PER_KV * W), bf16)
    kk_first = jnp.concatenate([kprev_ref[...], k_a[:W]], axis=0)
    vvt_first = jnp.concatenate([vtprev_ref[...], vt_a[:, :W]], axis=1)
    kprev_ref[...] = k_a[T - W:]
    vtprev_ref[...] = vt_a[:, T - W:]

    def attn_unit(b, hk):
        t0 = b * W
        kk = (kk_first if b == 0 else k_a[t0 - W:t0 + W]).astype(bf16)
        vvt = (vvt_first if b == 0 else vt_a[:, t0 - W:t0 + W]).astype(bf16)
        h0 = hk * Q_PER_KV
        qt = jnp.concatenate(
            [proj_t[T_QA + (h0 + g) * HEAD_DIM:T_QA + (h0 + g + 1) * HEAD_DIM, t0:t0 + W]
             for g in range(Q_PER_KV)], axis=1).astype(bf16)
        rhs = jnp.concatenate([qt, zeros_half] if hk == 0 else [zeros_half, qt], axis=0)
        st = _dot(kk, rhs)
        s_prev = st[:W] + first_block_penalty if b == 0 else st[:W]
        s = jnp.where(from_prev, s_prev, st[W:])
        sink = sink_ref[hk:hk + 1, :]
        m = jnp.maximum(jnp.max(s, axis=0, keepdims=True), sink)
        p = jnp.exp(s - m)
        denom = jnp.sum(p, axis=0, keepdims=True) + jnp.exp(sink - m)
        pt = jnp.concatenate([jnp.where(from_prev, p, 0.0).astype(bf16),
                              jnp.where(from_prev, 0.0, p).astype(bf16)], axis=0)
        out_t = _dot(vvt[hk * HEAD_DIM:(hk + 1) * HEAD_DIM, :], pt) * (1.0 / denom)
        for g in range(Q_PER_KV):
            r0 = (h0 + g) * HEAD_DIM
            mixt_ref[r0:r0 + HEAD_DIM, t0:t0 + W] = out_t[:, g * W:(g + 1) * W].astype(bf16)

    lane = lax.broadcasted_iota(jnp.int32, (1, LANES), 1)
    lane_even = (lane & 1) == 0
    cos_b = cosb_ref[pl.ds(blk, 1), :]
    sin_b = sinb_ref[pl.ds(blk, 1), :]
    cos_i, sin_i = cosi_ref[...], sini_ref[...]
    cos = cos_b * cos_i - sin_b * sin_i
    sin = sin_b * cos_i + cos_b * sin_i
    sin_signed = jnp.where(lane_even, -sin, sin)

    def rotate(t):
        swapped = jnp.where(lane_even, pltpu.roll(t, LANES - 1, 1), pltpu.roll(t, 1, 1))
        return t * cos + swapped * sin_signed

    def ret_unit(hh):
        c0 = hh * RET_HEAD_DIM
        q = rotate(proj_s[:, S_QR + c0:S_QR + c0 + RET_HEAD_DIM])
        k = rotate(proj_s[:, S_KR + c0:S_KR + c0 + RET_HEAD_DIM])
        vt = proj_t[T_VR + c0:T_VR + c0 + RET_HEAD_DIM, :].astype(bf16)
        gate_t = proj_t[T_GR + c0:T_GR + c0 + RET_HEAD_DIM, :]
        inner_t = _dot_nt(k.astype(bf16), q.astype(bf16)) * decay_ref[hh]
        state_t = state_ref[hh]
        o_t = (_dot(vt, inner_t.astype(bf16))
               + _dot_nt(state_t.astype(bf16), (q * xi_ref[hh]).astype(bf16)))
        state_ref[hh] = chunk_decay[hh] * state_t + _dot(vt, (k * zeta_ref[hh]).astype(bf16))
        mu = jnp.mean(o_t, axis=0, keepdims=True)
        d = o_t - mu
        var = jnp.mean(d * d, axis=0, keepdims=True)
        res = d * lax.rsqrt(var + GN_EPS) * (gate_t * (1.0 / (1.0 + jnp.exp(-gate_t))))
        mixt_ref[ATTN_W + c0:ATTN_W + c0 + RET_HEAD_DIM, :] = res.astype(bf16)

    mixer_units = ([functools.partial(attn_unit, b, hk) for b in range(T // W) for hk in range(N_KV_HEADS)]
                   + [functools.partial(ret_unit, hh) for hh in range(N_RET_HEADS)])
    def ffn_epilogue():
        o_ref[...] = x1_ref[...] + _rms(o_ref[...], g4_ref[...])

    down_units = [functools.partial(ffn_down_unit, n) for n in range(D_MODEL // MXU_TILE)]
    order = _interleave(down_units, mixer_units)
    order.insert(order.index(down_units[-1]) + 1, ffn_epilogue)
    for unit in order:
        unit()

    mixed = _dot_tn(mixt_ref[...], w_out_ref[...])
    x1_ref[...] = x + _rms(mixed, g2_ref[...])


def kernel(x, mix_pre_norm, w_in, attn_sinks, w_out, mix_post_norm, ffn_pre_norm, w_up,
           conv_w, conv_b, w_down, ffn_post_norm):
    batch, seq, d_model = x.shape
    depth = w_in.shape[0]
    assert d_model == D_MODEL and seq % SEQ_BLOCK == 0
    tabs = _retention_tables(seq)
    n_steps = seq // SEQ_BLOCK
    bf16 = jnp.bfloat16

    def resident(shape):
        nd = len(shape)
        return pl.BlockSpec(shape, lambda i, _n=nd: (0,) * _n, pipeline_mode=pl.Buffered(1))

    in_specs = [
        pl.BlockSpec((SEQ_BLOCK, D_MODEL), lambda i: (jnp.minimum(i, n_steps - 1), 0)),
        resident((1, D_MODEL)), resident((1, D_MODEL)), resident((1, D_MODEL)), resident((1, D_MODEL)),
        resident((N_KV_HEADS, Q_PER_KV * WINDOW)),
        resident((T_ROWS, D_MODEL)), resident((D_MODEL, S_COLS)),
        resident((D_MODEL, D_MODEL)),
        resident((D_MODEL, 2 * D_FF)), resident((D_FF, D_MODEL)),
        resident((3, 2 * D_FF)), resident((1, 2 * D_FF)),
        resident((n_steps, LANES)), resident((n_steps, LANES)),
        resident((SEQ_BLOCK, LANES)), resident((SEQ_BLOCK, LANES)),
        resident((N_RET_HEADS, SEQ_BLOCK, SEQ_BLOCK)),
        resident((N_RET_HEADS, SEQ_BLOCK, LANES)), resident((N_RET_HEADS, SEQ_BLOCK, LANES)),
    ]
    call = pl.pallas_call(
        functools.partial(_layer_kernel, tabs["chunk_decay"], n_steps),
        grid=(n_steps + 1,),
        in_specs=in_specs,
        out_specs=pl.BlockSpec((SEQ_BLOCK, D_MODEL), lambda i: (jnp.maximum(i - 1, 0), 0)),
        out_shape=jax.ShapeDtypeStruct((seq, D_MODEL), jnp.float32),
        scratch_shapes=[
            pltpu.VMEM((WINDOW, KV_W), jnp.float32),
            pltpu.VMEM((KV_W, WINDOW), jnp.float32),
            pltpu.VMEM((N_RET_HEADS, RET_HEAD_DIM, RET_HEAD_DIM), jnp.float32),
            pltpu.VMEM((2 * D_FF // LANES, CONV_ROW0 + 2 * SEQ_BLOCK, LANES), jnp.float32),
            pltpu.VMEM((D_MODEL, SEQ_BLOCK), bf16),
            pltpu.VMEM((SEQ_BLOCK, D_FF), bf16),
            pltpu.VMEM((SEQ_BLOCK, D_MODEL), jnp.float32),
        ],
        compiler_params=pltpu.CompilerParams(
            dimension_semantics=("arbitrary",), vmem_limit_bytes=VMEM_LIMIT_BYTES),
        name="hybrid_layer",
    )

    outs = []
    for bi in range(batch):
        xb = x[bi]
        for l in range(depth):
            wl = w_in[l]
            col = lambda a, n: wl[:, a:a + n]
            q_a, k_a, v_a = col(0, ATTN_W), col(ATTN_W, KV_W), col(ATTN_W + KV_W, KV_W)
            r0 = ATTN_W + 2 * KV_W
            q_r, k_r, v_r, g_r = (col(r0 + i * RET_W, RET_W) for i in range(4))
            w_t = jnp.concatenate([q_a * (HEAD_DIM ** -0.5), v_a, v_r, g_r], axis=1).T.astype(bf16)
            w_s = jnp.concatenate([k_a, q_r, k_r], axis=1).astype(bf16)
            sink_rows = jnp.repeat(attn_sinks[l].reshape(N_KV_HEADS, Q_PER_KV), WINDOW, axis=1)
            xb = call(
                xb,
                mix_pre_norm[l].reshape(1, D_MODEL), mix_post_norm[l].reshape(1, D_MODEL),
                ffn_pre_norm[l].reshape(1, D_MODEL), ffn_post_norm[l].reshape(1, D_MODEL),
                sink_rows,
                w_t, w_s, w_out[l].astype(bf16),
                w_up[l].astype(bf16), w_down[l].astype(bf16),
                conv_w[l], conv_b[l].reshape(1, 2 * D_FF),
                tabs["cos_base"], tabs["sin_base"], tabs["cos_intra"], tabs["sin_intra"],
                tabs["decay_t"], tabs["xi"], tabs["zeta"],
            )
        outs.append(xb)
    return outs[0][None] if batch == 1 else jnp.stack(outs, axis=0)
```

```python
import functools

import numpy as np
import jax
import jax.numpy as jnp
from jax import lax
from jax.experimental import pallas as pl
from jax.experimental.pallas import tpu as pltpu

D_MODEL = 1024
HEAD_DIM = 64
ATTN_W = 512
N_ATTN_HEADS = 8
N_KV_HEADS = 2
Q_PER_KV = N_ATTN_HEADS // N_KV_HEADS
KV_W = N_KV_HEADS * HEAD_DIM
WINDOW = 128
RET_W = 512
N_RET_HEADS = 4
RET_HEAD_DIM = 128
D_FF = 2816
RMS_EPS = 1e-6
GN_EPS = 1e-6
MASK_VALUE = -1e30
GELU_C = 0.7978845608028654

LANES = 128
SUBLANES = 8
MXU_TILE = 256
SEQ_BLOCK = 256
FF_TILE = MXU_TILE
N_FF_TILES = D_FF // FF_TILE
CONV_ROW0 = 2 * SUBLANES
VMEM_LIMIT_BYTES = 56 * 1024 * 1024

T_QA, T_VA, T_VR, T_GR = 0, ATTN_W, ATTN_W + KV_W, ATTN_W + KV_W + RET_W
T_ROWS = ATTN_W + KV_W + 2 * RET_W
S_KA, S_QR, S_KR = 0, KV_W, KV_W + RET_W
S_COLS = KV_W + 2 * RET_W


def _retention_tables(seq):
    c = SEQ_BLOCK
    n_steps = seq // c
    angle = 1.0 / np.power(10000.0, np.linspace(0.0, 1.0, RET_HEAD_DIM // 2))
    angle = np.repeat(angle, 2)
    base = (np.arange(n_steps, dtype=np.float64) * c)[:, None] * angle[None]
    intra = np.arange(c, dtype=np.float64)[:, None] * angle[None]
    gamma = 1.0 - np.power(2.0, -5.0 - np.arange(N_RET_HEADS, dtype=np.float64))
    k_scale = RET_HEAD_DIM ** -0.5
    idx = np.arange(c, dtype=np.float64)
    rel = idx[None, :] - idx[:, None]
    decay_t = np.where(rel[None] >= 0, gamma[:, None, None] ** np.maximum(rel, 0.0)[None], 0.0)
    xi = gamma[:, None] ** (idx[None, :] + 1.0)
    zeta = gamma[:, None] ** (c - 1.0 - idx[None, :])
    f32 = lambda a: jnp.asarray(a, dtype=jnp.float32)
    return dict(
        cos_base=f32(np.cos(base)), sin_base=f32(np.sin(base)),
        cos_intra=f32(np.cos(intra)), sin_intra=f32(np.sin(intra)),
        decay_t=f32(k_scale * decay_t),
        xi=f32(np.broadcast_to(xi[:, :, None], (N_RET_HEADS, c, LANES))),
        zeta=f32(np.broadcast_to(k_scale * zeta[:, :, None], (N_RET_HEADS, c, LANES))),
        chunk_decay=[float(g ** c) for g in gamma],
    )


def _rms(x, w):
    return x * lax.rsqrt(jnp.mean(x * x, axis=-1, keepdims=True) + RMS_EPS) * w


def _dot(a, b):
    return jnp.dot(a, b, preferred_element_type=jnp.float32)


def _dot_nt(a, b):
    return lax.dot_general(a, b, (((1,), (1,)), ((), ())), preferred_element_type=jnp.float32)


def _dot_tn(a, b):
    return lax.dot_general(a, b, (((0,), (0,)), ((), ())), preferred_element_type=jnp.float32)


def _interleave(primary, filler):
    order, used = [], 0
    for i, unit in enumerate(primary):
        order.append(unit)
        upto = (i + 1) * len(filler) // len(primary)
        order.extend(filler[used:upto])
        used = upto
    return order


def _layer_kernel(chunk_decay, n_blocks,
                  x_ref, xn_ref, g1_ref, g2_ref, g3_ref, g4_ref, sink_ref,
                  w_t_ref, w_s_ref, w_out_ref, w_up_ref, w_down_ref, cw_ref, cb_ref,
                  cosb_ref, sinb_ref, cosi_ref, sini_ref, decay_ref, xi_ref, zeta_ref,
                  o_ref,
                  kprev_ref, vtprev_ref, state_ref, pbuf_ref, mixt_ref, y_ref, x1_ref, h_ref):
    step = pl.program_id(0)
    bf16 = jnp.bfloat16
    T = SEQ_BLOCK
    W = WINDOW

    @pl.when(step == 0)
    def _init():
        kprev_ref[...] = jnp.zeros_like(kprev_ref)
        vtprev_ref[...] = jnp.zeros_like(vtprev_ref)
        state_ref[...] = jnp.zeros_like(state_ref)
        pbuf_ref[...] = jnp.zeros_like(pbuf_ref)
        x1_ref[...] = jnp.zeros_like(x1_ref)
        h_ref[...] = _rms(x_ref[...], g1_ref[...]).astype(bf16)

    blk = jnp.minimum(step, n_blocks - 1)

    h = h_ref[...]
    proj_t = _dot_nt(w_t_ref[...], h)
    proj_s = _dot(h, w_s_ref[...])
    h2 = _rms(x1_ref[...], g3_ref[...]).astype(bf16)

    def conv(p, slab, c0):
        pbuf_ref[slab, pl.ds(CONV_ROW0, T, stride=2), :] = p
        p1 = pbuf_ref[slab, pl.ds(CONV_ROW0 - 2, T, stride=2), :]
        p2 = pbuf_ref[slab, pl.ds(CONV_ROW0 - 4, T, stride=2), :]
        pbuf_ref[slab, CONV_ROW0 - 4:CONV_ROW0, :] = pbuf_ref[slab, CONV_ROW0 + 2 * T - 4:CONV_ROW0 + 2 * T, :]
        return (cw_ref[2:3, c0:c0 + LANES] * p + cw_ref[1:2, c0:c0 + LANES] * p1
                + cw_ref[0:1, c0:c0 + LANES] * p2 + cb_ref[:, c0:c0 + LANES])

    def ffn_up_unit(j):
        c_gate, c_val = j * FF_TILE, D_FF + j * FF_TILE
        p_gate = _dot(h2, w_up_ref[:, c_gate:c_gate + FF_TILE])
        p_val = _dot(h2, w_up_ref[:, c_val:c_val + FF_TILE])
        for half in range(FF_TILE // LANES):
            l0 = half * LANES
            ug = conv(p_gate[:, l0:l0 + LANES], (c_gate + l0) // LANES, c_gate + l0)
            uv = conv(p_val[:, l0:l0 + LANES], (c_val + l0) // LANES, c_val + l0)
            t = jnp.tanh(ug * (ug * ug * (GELU_C * 0.044715) + GELU_C))
            y_ref[:, c_gate + l0:c_gate + l0 + LANES] = (ug * (1.0 + t) * uv).astype(bf16)

    def ffn_down_unit(n):
        c0 = n * MXU_TILE
        o_ref[:, c0:c0 + MXU_TILE] = _dot(y_ref[...], w_down_ref[:, c0:c0 + MXU_TILE])

    for j in range(N_FF_TILES):
        ffn_up_unit(j)

    key_idx = lax.broadcasted_iota(jnp.int32, (W, Q_PER_KV * W), 0)
    qry_idx = lax.broadcasted_iota(jnp.int32, (W, Q_PER_KV * W), 1) & (W - 1)
    from_prev = key_idx > qry_idx
    first_block_penalty = jnp.where(step == 0, MASK_VALUE, 0.0)
    k_a = proj_s[:, S_KA:S_KA + KV_W]
    vt_a = proj_t[T_VA:T_VA + KV_W, :]
    zeros_half = jnp.zeros((HEAD_DIM, Q_PER_KV * W), bf16)
    kk_first = jnp.concatenate([kprev_ref[...], k_a[:W]], axis=0)
    vvt_first = jnp.concatenate([vtprev_ref[...], vt_a[:, :W]], axis=1)
    kprev_ref[...] = k_a[T - W:]
    vtprev_ref[...] = vt_a[:, T - W:]

    def attn_front(b, hk):
        t0 = b * W
        kk = (kk_first if b == 0 else k_a[t0 - W:t0 + W]).astype(bf16)
        h0 = hk * Q_PER_KV
        qt = jnp.concatenate(
            [proj_t[T_QA + (h0 + g) * HEAD_DIM:T_QA + (h0 + g + 1) * HEAD_DIM, t0:t0 + W]
             for g in range(Q_PER_KV)], axis=1).astype(bf16)
        rhs = jnp.concatenate([qt, zeros_half] if hk == 0 else [zeros_half, qt], axis=0)
        return _dot(kk, rhs)

    def attn_back(b, hk, st):
        t0 = b * W
        vvt = (vvt_first if b == 0 else vt_a[:, t0 - W:t0 + W]).astype(bf16)
        h0 = hk * Q_PER_KV
        s_prev = st[:W] + first_block_penalty if b == 0 else st[:W]
        s = jnp.where(from_prev, s_prev, st[W:])
        sink = sink_ref[hk:hk + 1, :]
        m = jnp.maximum(jnp.max(s, axis=0, keepdims=True), sink)
        p = jnp.exp(s - m)
        denom = jnp.sum(p, axis=0, keepdims=True) + jnp.exp(sink - m)
        pt = jnp.concatenate([jnp.where(from_prev, p, 0.0).astype(bf16),
                              jnp.where(from_prev, 0.0, p).astype(bf16)], axis=0)
        out_t = _dot(vvt[hk * HEAD_DIM:(hk + 1) * HEAD_DIM, :], pt) * (1.0 / denom)
        for g in range(Q_PER_KV):
            r0 = (h0 + g) * HEAD_DIM
            mixt_ref[r0:r0 + HEAD_DIM, t0:t0 + W] = out_t[:, g * W:(g + 1) * W].astype(bf16)

    lane = lax.broadcasted_iota(jnp.int32, (1, LANES), 1)
    lane_even = (lane & 1) == 0
    cos_b = cosb_ref[pl.ds(blk, 1), :]
    sin_b = sinb_ref[pl.ds(blk, 1), :]
    cos_i, sin_i = cosi_ref[...], sini_ref[...]
    cos = cos_b * cos_i - sin_b * sin_i
    sin = sin_b * cos_i + cos_b * sin_i
    sin_signed = jnp.where(lane_even, -sin, sin)

    def rotate(t):
        swapped = jnp.where(lane_even, pltpu.roll(t, LANES - 1, 1), pltpu.roll(t, 1, 1))
        return t * cos + swapped * sin_signed

    def ret_front(hh):
        c0 = hh * RET_HEAD_DIM
        q = rotate(proj_s[:, S_QR + c0:S_QR + c0 + RET_HEAD_DIM])
        k = rotate(proj_s[:, S_KR + c0:S_KR + c0 + RET_HEAD_DIM])
        return q, k, _dot_nt(k.astype(bf16), q.astype(bf16))

    def ret_back(hh, q, k, kq_t):
        c0 = hh * RET_HEAD_DIM
        vt = proj_t[T_VR + c0:T_VR + c0 + RET_HEAD_DIM, :].astype(bf16)
        gate_t = proj_t[T_GR + c0:T_GR + c0 + RET_HEAD_DIM, :]
        inner_t = kq_t * decay_ref[hh]
        state_t = state_ref[hh]
        o_t = (_dot(vt, inner_t.astype(bf16))
               + _dot_nt(state_t.astype(bf16), (q * xi_ref[hh]).astype(bf16)))
        state_ref[hh] = chunk_decay[hh] * state_t + _dot(vt, (k * zeta_ref[hh]).astype(bf16))
        mu = jnp.mean(o_t, axis=0, keepdims=True)
        d = o_t - mu
        var = jnp.mean(d * d, axis=0, keepdims=True)
        res = d * lax.rsqrt(var + GN_EPS) * (gate_t * (1.0 / (1.0 + jnp.exp(-gate_t))))
        mixt_ref[ATTN_W + c0:ATTN_W + c0 + RET_HEAD_DIM, :] = res.astype(bf16)

    def ffn_epilogue():
        o_ref[...] = x1_ref[...] + _rms(o_ref[...], g4_ref[...])

    def next_input_norm():
        h_ref[...] = _rms(xn_ref[...], g1_ref[...]).astype(bf16)

    attn_ids = [(b, hk) for b in range(T // W) for hk in range(N_KV_HEADS)]
    pairs = ([(attn_front, attn_back, ids) for ids in (attn_ids[0:2], attn_ids[2:4])]
             + [(ret_front, ret_back, ids) for ids in ([(0,), (1,)], [(2,), (3,)])])
    for n, (front, back, ids) in enumerate(pairs):
        fronts = [front(*i) for i in ids]
        ffn_down_unit(n)
        for i, f in zip(ids, fronts):
            back(*i, *(f if isinstance(f, tuple) else (f,)))
        if n == len(pairs) - 2:
            next_input_norm()
    ffn_epilogue()

    mixed = _dot_tn(mixt_ref[...], w_out_ref[...])
    x1_ref[...] = x_ref[...] + _rms(mixed, g2_ref[...])


def kernel(x, mix_pre_norm, w_in, attn_sinks, w_out, mix_post_norm, ffn_pre_norm, w_up,
           conv_w, conv_b, w_down, ffn_post_norm):
    batch, seq, d_model = x.shape
    depth = w_in.shape[0]
    assert d_model == D_MODEL and seq % SEQ_BLOCK == 0
    tabs = _retention_tables(seq)
    n_steps = seq // SEQ_BLOCK
    bf16 = jnp.bfloat16

    def resident(shape):
        nd = len(shape)
        return pl.BlockSpec(shape, lambda i, _n=nd: (0,) * _n, pipeline_mode=pl.Buffered(1))

    in_specs = [
        pl.BlockSpec((SEQ_BLOCK, D_MODEL), lambda i: (jnp.minimum(i, n_steps - 1), 0)),
        pl.BlockSpec((SEQ_BLOCK, D_MODEL), lambda i: (jnp.minimum(i + 1, n_steps - 1), 0)),
        resident((1, D_MODEL)), resident((1, D_MODEL)), resident((1, D_MODEL)), resident((1, D_MODEL)),
        resident((N_KV_HEADS, Q_PER_KV * WINDOW)),
        resident((T_ROWS, D_MODEL)), resident((D_MODEL, S_COLS)),
        resident((D_MODEL, D_MODEL)),
        resident((D_MODEL, 2 * D_FF)), resident((D_FF, D_MODEL)),
        resident((3, 2 * D_FF)), resident((1, 2 * D_FF)),
        resident((n_steps, LANES)), resident((n_steps, LANES)),
        resident((SEQ_BLOCK, LANES)), resident((SEQ_BLOCK, LANES)),
        resident((N_RET_HEADS, SEQ_BLOCK, SEQ_BLOCK)),
        resident((N_RET_HEADS, SEQ_BLOCK, LANES)), resident((N_RET_HEADS, SEQ_BLOCK, LANES)),
    ]
    call = pl.pallas_call(
        functools.partial(_layer_kernel, tabs["chunk_decay"], n_steps),
        grid=(n_steps + 1,),
        in_specs=in_specs,
        out_specs=pl.BlockSpec((SEQ_BLOCK, D_MODEL), lambda i: (jnp.maximum(i - 1, 0), 0)),
        out_shape=jax.ShapeDtypeStruct((seq, D_MODEL), jnp.float32),
        scratch_shapes=[
            pltpu.VMEM((WINDOW, KV_W), jnp.float32),
            pltpu.VMEM((KV_W, WINDOW), jnp.float32),
            pltpu.VMEM((N_RET_HEADS, RET_HEAD_DIM, RET_HEAD_DIM), jnp.float32),
            pltpu.VMEM((2 * D_FF // LANES, CONV_ROW0 + 2 * SEQ_BLOCK, LANES), jnp.float32),
            pltpu.VMEM((D_MODEL, SEQ_BLOCK), bf16),
            pltpu.VMEM((SEQ_BLOCK, D_FF), bf16),
            pltpu.VMEM((SEQ_BLOCK, D_MODEL), jnp.float32),
            pltpu.VMEM((SEQ_BLOCK, D_MODEL), bf16),
        ],
        compiler_params=pltpu.CompilerParams(
            dimension_semantics=("arbitrary",), vmem_limit_bytes=VMEM_LIMIT_BYTES),
        name="hybrid_layer",
    )

    outs = []
    for bi in range(batch):
        xb = x[bi]
        for l in range(depth):
            wl = w_in[l]
            col = lambda a, n: wl[:, a:a + n]
            q_a, k_a, v_a = col(0, ATTN_W), col(ATTN_W, KV_W), col(ATTN_W + KV_W, KV_W)
            r0 = ATTN_W + 2 * KV_W
            q_r, k_r, v_r, g_r = (col(r0 + i * RET_W, RET_W) for i in range(4))
            w_t = jnp.concatenate([q_a * (HEAD_DIM ** -0.5), v_a, v_r, g_r], axis=1).T.astype(bf16)
            w_s = jnp.concatenate([k_a, q_r, k_r], axis=1).astype(bf16)
            sink_rows = jnp.repeat(attn_sinks[l].reshape(N_KV_HEADS, Q_PER_KV), WINDOW, axis=1)
            xb = call(
                xb, xb,
                mix_pre_norm[l].reshape(1, D_MODEL), mix_post_norm[l].reshape(1, D_MODEL),
                ffn_pre_norm[l].reshape(1, D_MODEL), ffn_post_norm[l].reshape(1, D_MODEL),
                sink_rows,
                w_t, w_s, w_out[l].astype(bf16),
                w_up[l].astype(bf16), (0.5 * w_down[l]).astype(bf16),
                conv_w[l], conv_b[l].reshape(1, 2 * D_FF),
                tabs["cos_base"], tabs["sin_base"], tabs["cos_intra"], tabs["sin_intra"],
                tabs["decay_t"], tabs["xi"], tabs["zeta"],
            )
        outs.append(xb)
    return outs[0][None] if batch == 1 else jnp.stack(outs, axis=0)
```

```python
import functools

import numpy as np
import jax
import jax.numpy as jnp
from jax import lax
from jax.experimental import pallas as pl
from jax.experimental.pallas import tpu as pltpu

D_MODEL = 1024
HEAD_DIM = 64
ATTN_W = 512
N_ATTN_HEADS = 8
N_KV_HEADS = 2
Q_PER_KV = N_ATTN_HEADS // N_KV_HEADS
KV_W = N_KV_HEADS * HEAD_DIM
WINDOW = 128
RET_W = 512
N_RET_HEADS = 4
RET_HEAD_DIM = 128
D_FF = 2816
RMS_EPS = 1e-6
GN_EPS = 1e-6
MASK_VALUE = -1e30
GELU_C = 0.7978845608028654

LANES = 128
SUBLANES = 8
MXU_TILE = 256
SEQ_BLOCK = 256
FF_TILE = MXU_TILE
N_FF_TILES = D_FF // FF_TILE
CONV_ROW0 = 2 * SUBLANES
STAGE_ROWS = SEQ_BLOCK
V7X_VMEM_BYTES = 64 * 1024 * 1024
VMEM_LIMIT_BYTES = V7X_VMEM_BYTES - 4 * 1024 * 1024

T_QA, T_VA, T_VR, T_GR = 0, ATTN_W, ATTN_W + KV_W, ATTN_W + KV_W + RET_W
T_ROWS = ATTN_W + KV_W + 2 * RET_W
S_KA, S_QR, S_KR = 0, KV_W, KV_W + RET_W
S_COLS = KV_W + 2 * RET_W


def _retention_tables(seq):
    c = SEQ_BLOCK
    n_steps = seq // c
    angle = 1.0 / np.power(10000.0, np.linspace(0.0, 1.0, RET_HEAD_DIM // 2))
    angle = np.repeat(angle, 2)
    base = (np.arange(n_steps, dtype=np.float64) * c)[:, None] * angle[None]
    intra = np.arange(c, dtype=np.float64)[:, None] * angle[None]
    gamma = 1.0 - np.power(2.0, -5.0 - np.arange(N_RET_HEADS, dtype=np.float64))
    k_scale = RET_HEAD_DIM ** -0.5
    idx = np.arange(c, dtype=np.float64)
    rel = idx[None, :] - idx[:, None]
    decay_t = np.where(rel[None] >= 0, gamma[:, None, None] ** np.maximum(rel, 0.0)[None], 0.0)
    xi = gamma[:, None] ** (idx[None, :] + 1.0)
    zeta = gamma[:, None] ** (c - 1.0 - idx[None, :])
    f32 = lambda a: jnp.asarray(a, dtype=jnp.float32)
    return dict(
        cos_base=f32(np.cos(base)), sin_base=f32(np.sin(base)),
        cos_intra=f32(np.cos(intra)), sin_intra=f32(np.sin(intra)),
        decay_t=f32(k_scale * decay_t),
        xi=f32(np.broadcast_to(xi[:, :, None], (N_RET_HEADS, c, LANES))),
        zeta=f32(np.broadcast_to(k_scale * zeta[:, :, None], (N_RET_HEADS, c, LANES))),
        chunk_decay=[float(g ** c) for g in gamma],
    )


def _rms(x, w):
    return x * lax.rsqrt(jnp.mean(x * x, axis=-1, keepdims=True) + RMS_EPS) * w


def _dot(a, b):
    return jnp.dot(a, b, preferred_element_type=jnp.float32)


def _dot_nt(a, b):
    return lax.dot_general(a, b, (((1,), (1,)), ((), ())), preferred_element_type=jnp.float32)


def _dot_tn(a, b):
    return lax.dot_general(a, b, (((0,), (0,)), ((), ())), preferred_element_type=jnp.float32)


def _interleave(primary, filler):
    order, used = [], 0
    for i, unit in enumerate(primary):
        order.append(unit)
        upto = (i + 1) * len(filler) // len(primary)
        order.extend(filler[used:upto])
        used = upto
    return order


def _stage_weights(w_in_hbm, w_out_hbm, w_up_hbm, w_down_hbm,
                   w_t_ref, w_s_ref, w_out_ref, w_up_ref, w_down_ref, stage_ref, sem):
    bf16 = jnp.bfloat16
    R = STAGE_ROWS

    def plain(dst_ref, dst_col, scale=None):
        def store(tile, row0):
            dst_ref[row0:row0 + R, dst_col:dst_col + LANES] = (tile if scale is None else tile * scale).astype(bf16)
        return store

    def transposed(dst_row, scale=None):
        def store(tile, row0):
            tile = tile if scale is None else tile * scale
            w_t_ref[dst_row:dst_row + LANES, row0:row0 + R] = jnp.transpose(tile).astype(bf16)
        return store

    in_slabs = ([transposed(T_QA + i * LANES, HEAD_DIM ** -0.5) for i in range(ATTN_W // LANES)]
                + [plain(w_s_ref, S_KA), transposed(T_VA)]
                + [plain(w_s_ref, S_QR + i * LANES) for i in range(RET_W // LANES)]
                + [plain(w_s_ref, S_KR + i * LANES) for i in range(RET_W // LANES)]
                + [transposed(T_VR + i * LANES) for i in range(RET_W // LANES)]
                + [transposed(T_GR + i * LANES) for i in range(RET_W // LANES)])
    tiles = []
    for src, stores in ((w_in_hbm, in_slabs),
                        (w_out_hbm, [plain(w_out_ref, c) for c in range(0, D_MODEL, LANES)]),
                        (w_up_hbm, [plain(w_up_ref, c) for c in range(0, 2 * D_FF, LANES)]),
                        (w_down_hbm, [plain(w_down_ref, c, 0.5) for c in range(0, D_MODEL, LANES)])):
        for row0 in range(0, src.shape[0], R):
            tiles += [(src, row0, i * LANES, store) for i, store in enumerate(stores)]
    n_slots = stage_ref.shape[0]
    waves = [tiles[i:i + n_slots] for i in range(0, len(tiles), n_slots)]

    def copy(w, slot):
        src, row0, col0, _ = waves[w][slot]
        return pltpu.make_async_copy(src.at[pl.ds(row0, R), pl.ds(col0, LANES)],
                                     stage_ref.at[slot, pl.ds((w % 2) * R, R), :], sem.at[w % 2])

    def start(w):
        for slot in range(len(waves[w])):
            copy(w, slot).start()

    for w in range(min(2, len(waves))):
        start(w)
    for w in range(len(waves)):
        for slot in range(len(waves[w])):
            copy(w, slot).wait()
        for slot, (_, row0, _, store) in enumerate(waves[w]):
            store(stage_ref[slot, (w % 2) * R:(w % 2 + 1) * R, :], row0)
        if w + 2 < len(waves):
            start(w + 2)


def _layer_kernel(chunk_decay, n_blocks,
                  x_ref, xn_ref, g1_ref, g2_ref, g3_ref, g4_ref, sink_ref,
                  w_in_hbm, w_out_hbm, w_up_hbm, w_down_hbm, cw_ref, cb_ref,
                  cosb_ref, sinb_ref, cosi_ref, sini_ref, decay_ref, xi_ref, zeta_ref,
                  o_ref,
                  w_t_ref, w_s_ref, w_out_ref, w_up_ref, w_down_ref, stage_sem,
                  kprev_ref, vtprev_ref, state_ref, pbuf_ref, mixt_ref, y_ref, x1_ref, h_ref):
    step = pl.program_id(0)
    bf16 = jnp.bfloat16
    T = SEQ_BLOCK
    W = WINDOW

    @pl.when(step == 0)
    def _init():
        _stage_weights(w_in_hbm, w_out_hbm, w_up_hbm, w_down_hbm,
                       w_t_ref, w_s_ref, w_out_ref, w_up_ref, w_down_ref, pbuf_ref, stage_sem)
        kprev_ref[...] = jnp.zeros_like(kprev_ref)
        vtprev_ref[...] = jnp.zeros_like(vtprev_ref)
        state_ref[...] = jnp.zeros_like(state_ref)
        pbuf_ref[...] = jnp.zeros_like(pbuf_ref)
        x1_ref[...] = jnp.zeros_like(x1_ref)
        h_ref[...] = _rms(x_ref[...], g1_ref[...]).astype(bf16)

    blk = jnp.minimum(step, n_blocks - 1)

    h = h_ref[...]
    proj_t = _dot_nt(w_t_ref[...], h)
    proj_s = _dot(h, w_s_ref[...])
    h2 = _rms(x1_ref[...], g3_ref[...]).astype(bf16)

    def conv(p, slab, c0):
        pbuf_ref[slab, pl.ds(CONV_ROW0, T, stride=2), :] = p
        p1 = pbuf_ref[slab, pl.ds(CONV_ROW0 - 2, T, stride=2), :]
        p2 = pbuf_ref[slab, pl.ds(CONV_ROW0 - 4, T, stride=2), :]
        pbuf_ref[slab, CONV_ROW0 - 4:CONV_ROW0, :] = pbuf_ref[slab, CONV_ROW0 + 2 * T - 4:CONV_ROW0 + 2 * T, :]
        return (cw_ref[2:3, c0:c0 + LANES] * p + cw_ref[1:2, c0:c0 + LANES] * p1
                + cw_ref[0:1, c0:c0 + LANES] * p2 + cb_ref[:, c0:c0 + LANES])

    def ffn_up_unit(j):
        c_gate, c_val = j * FF_TILE, D_FF + j * FF_TILE
        p_gate = _dot(h2, w_up_ref[:, c_gate:c_gate + FF_TILE])
        p_val = _dot(h2, w_up_ref[:, c_val:c_val + FF_TILE])
        for half in range(FF_TILE // LANES):
            l0 = half * LANES
            ug = conv(p_gate[:, l0:l0 + LANES], (c_gate + l0) // LANES, c_gate + l0)
            uv = conv(p_val[:, l0:l0 + LANES], (c_val + l0) // LANES, c_val + l0)
            t = jnp.tanh(ug * (ug * ug * (GELU_C * 0.044715) + GELU_C))
            y_ref[:, c_gate + l0:c_gate + l0 + LANES] = (ug * (1.0 + t) * uv).astype(bf16)

    def ffn_down_unit(n):
        c0 = n * MXU_TILE
        o_ref[:, c0:c0 + MXU_TILE] = _dot(y_ref[...], w_down_ref[:, c0:c0 + MXU_TILE])

    for j in range(N_FF_TILES):
        ffn_up_unit(j)

    key_idx = lax.broadcasted_iota(jnp.int32, (W, Q_PER_KV * W), 0)
    qry_idx = lax.broadcasted_iota(jnp.int32, (W, Q_PER_KV * W), 1) & (W - 1)
    from_prev = key_idx > qry_idx
    first_block_penalty = jnp.where(step == 0, MASK_VALUE, 0.0)
    k_a = proj_s[:, S_KA:S_KA + KV_W]
    vt_a = proj_t[T_VA:T_VA + KV_W, :]
    zeros_half = jnp.zeros((HEAD_DIM, Q_PER_KV * W), bf16)
    kk_first = jnp.concatenate([kprev_ref[...], k_a[:W]], axis=0)
    vvt_first = jnp.concatenate([vtprev_ref[...], vt_a[:, :W]], axis=1)
    kprev_ref[...] = k_a[T - W:]
    vtprev_ref[...] = vt_a[:, T - W:]

    def attn_front(b, hk):
        t0 = b * W
        kk = (kk_first if b == 0 else k_a[t0 - W:t0 + W]).astype(bf16)
        h0 = hk * Q_PER_KV
        qt = jnp.concatenate(
            [proj_t[T_QA + (h0 + g) * HEAD_DIM:T_QA + (h0 + g + 1) * HEAD_DIM, t0:t0 + W]
             for g in range(Q_PER_KV)], axis=1).astype(bf16)
        rhs = jnp.concatenate([qt, zeros_half] if hk == 0 else [zeros_half, qt], axis=0)
        return _dot(kk, rhs)

    def attn_back(b, hk, st):
        t0 = b * W
        vvt = (vvt_first if b == 0 else vt_a[:, t0 - W:t0 + W]).astype(bf16)
        h0 = hk * Q_PER_KV
        s_prev = st[:W] + first_block_penalty if b == 0 else st[:W]
        s = jnp.where(from_prev, s_prev, st[W:])
        sink = sink_ref[hk:hk + 1, :]
        m = jnp.maximum(jnp.max(s, axis=0, keepdims=True), sink)
        p = jnp.exp(s - m)
        denom = jnp.sum(p, axis=0, keepdims=True) + jnp.exp(sink - m)
        pt = jnp.concatenate([jnp.where(from_prev, p, 0.0).astype(bf16),
                              jnp.where(from_prev, 0.0, p).astype(bf16)], axis=0)
        out_t = _dot(vvt[hk * HEAD_DIM:(hk + 1) * HEAD_DIM, :], pt) * (1.0 / denom)
        for g in range(Q_PER_KV):
            r0 = (h0 + g) * HEAD_DIM
            mixt_ref[r0:r0 + HEAD_DIM, t0:t0 + W] = out_t[:, g * W:(g + 1) * W].astype(bf16)

    lane = lax.broadcasted_iota(jnp.int32, (1, LANES), 1)
    lane_even = (lane & 1) == 0
    cos_b = cosb_ref[pl.ds(blk, 1), :]
    sin_b = sinb_ref[pl.ds(blk, 1), :]
    cos_i, sin_i = cosi_ref[...], sini_ref[...]
    cos = cos_b * cos_i - sin_b * sin_i
    sin = sin_b * cos_i + cos_b * sin_i
    sin_signed = jnp.where(lane_even, -sin, sin)

    def rotate(t):
        swapped = jnp.where(lane_even, pltpu.roll(t, LANES - 1, 1), pltpu.roll(t, 1, 1))
        return t * cos + swapped * sin_signed

    def ret_front(hh):
        c0 = hh * RET_HEAD_DIM
        q = rotate(proj_s[:, S_QR + c0:S_QR + c0 + RET_HEAD_DIM])
        k = rotate(proj_s[:, S_KR + c0:S_KR + c0 + RET_HEAD_DIM])
        return q, k, _dot_nt(k.astype(bf16), q.astype(bf16))

    def ret_back(hh, q, k, kq_t):
        c0 = hh * RET_HEAD_DIM
        vt = proj_t[T_VR + c0:T_VR + c0 + RET_HEAD_DIM, :].astype(bf16)
        gate_t = proj_t[T_GR + c0:T_GR + c0 + RET_HEAD_DIM, :]
        inner_t = kq_t * decay_ref[hh]
        state_t = state_ref[hh]
        o_t = (_dot(vt, inner_t.astype(bf16))
               + _dot_nt(state_t.astype(bf16), (q * xi_ref[hh]).astype(bf16)))
        state_ref[hh] = chunk_decay[hh] * state_t + _dot(vt, (k * zeta_ref[hh]).astype(bf16))
        mu = jnp.mean(o_t, axis=0, keepdims=True)
        d = o_t - mu
        var = jnp.mean(d * d, axis=0, keepdims=True)
        res = d * lax.rsqrt(var + GN_EPS) * (gate_t * (1.0 / (1.0 + jnp.exp(-gate_t))))
        mixt_ref[ATTN_W + c0:ATTN_W + c0 + RET_HEAD_DIM, :] = res.astype(bf16)

    def ffn_epilogue():
        o_ref[...] = x1_ref[...] + _rms(o_ref[...], g4_ref[...])

    def next_input_norm():
        h_ref[...] = _rms(xn_ref[...], g1_ref[...]).astype(bf16)

    attn_ids = [(b, hk) for b in range(T // W) for hk in range(N_KV_HEADS)]
    pairs = ([(attn_front, attn_back, ids) for ids in (attn_ids[0:2], attn_ids[2:4])]
             + [(ret_front, ret_back, ids) for ids in ([(0,), (1,)], [(2,), (3,)])])
    for n, (front, back, ids) in enumerate(pairs):
        fronts = [front(*i) for i in ids]
        ffn_down_unit(n)
        for i, f in zip(ids, fronts):
            back(*i, *(f if isinstance(f, tuple) else (f,)))
        if n == len(pairs) - 2:
            next_input_norm()
    ffn_epilogue()

    mixed = _dot_tn(mixt_ref[...], w_out_ref[...])
    x1_ref[...] = x_ref[...] + _rms(mixed, g2_ref[...])


def kernel(x, mix_pre_norm, w_in, attn_sinks, w_out, mix_post_norm, ffn_pre_norm, w_up,
           conv_w, conv_b, w_down, ffn_post_norm):
    batch, seq, d_model = x.shape
    depth = w_in.shape[0]
    assert d_model == D_MODEL and seq % SEQ_BLOCK == 0
    tabs = _retention_tables(seq)
    n_steps = seq // SEQ_BLOCK
    bf16 = jnp.bfloat16

    def resident(shape):
        nd = len(shape)
        return pl.BlockSpec(shape, lambda i, _n=nd: (0,) * _n, pipeline_mode=pl.Buffered(1))

    in_specs = [
        pl.BlockSpec((SEQ_BLOCK, D_MODEL), lambda i: (jnp.minimum(i, n_steps - 1), 0)),
        pl.BlockSpec((SEQ_BLOCK, D_MODEL), lambda i: (jnp.minimum(i + 1, n_steps - 1), 0)),
        resident((1, D_MODEL)), resident((1, D_MODEL)), resident((1, D_MODEL)), resident((1, D_MODEL)),
        resident((N_KV_HEADS, Q_PER_KV * WINDOW)),
        pl.BlockSpec(memory_space=pl.ANY), pl.BlockSpec(memory_space=pl.ANY),
        pl.BlockSpec(memory_space=pl.ANY), pl.BlockSpec(memory_space=pl.ANY),
        resident((3, 2 * D_FF)), resident((1, 2 * D_FF)),
        resident((n_steps, LANES)), resident((n_steps, LANES)),
        resident((SEQ_BLOCK, LANES)), resident((SEQ_BLOCK, LANES)),
        resident((N_RET_HEADS, SEQ_BLOCK, SEQ_BLOCK)),
        resident((N_RET_HEADS, SEQ_BLOCK, LANES)), resident((N_RET_HEADS, SEQ_BLOCK, LANES)),
    ]
    call = pl.pallas_call(
        functools.partial(_layer_kernel, tabs["chunk_decay"], n_steps),
        grid=(n_steps + 1,),
        in_specs=in_specs,
        out_specs=pl.BlockSpec((SEQ_BLOCK, D_MODEL), lambda i: (jnp.maximum(i - 1, 0), 0)),
        out_shape=jax.ShapeDtypeStruct((seq, D_MODEL), jnp.float32),
        scratch_shapes=[
            pltpu.VMEM((T_ROWS, D_MODEL), bf16),
            pltpu.VMEM((D_MODEL, S_COLS), bf16),
            pltpu.VMEM((D_MODEL, D_MODEL), bf16),
            pltpu.VMEM((D_MODEL, 2 * D_FF), bf16),
            pltpu.VMEM((D_FF, D_MODEL), bf16),
            pltpu.SemaphoreType.DMA((2,)),
            pltpu.VMEM((WINDOW, KV_W), jnp.float32),
            pltpu.VMEM((KV_W, WINDOW), jnp.float32),
            pltpu.VMEM((N_RET_HEADS, RET_HEAD_DIM, RET_HEAD_DIM), jnp.float32),
            pltpu.VMEM((2 * D_FF // LANES, CONV_ROW0 + 2 * SEQ_BLOCK, LANES), jnp.float32),
            pltpu.VMEM((D_MODEL, SEQ_BLOCK), bf16),
            pltpu.VMEM((SEQ_BLOCK, D_FF), bf16),
            pltpu.VMEM((SEQ_BLOCK, D_MODEL), jnp.float32),
            pltpu.VMEM((SEQ_BLOCK, D_MODEL), bf16),
        ],
        compiler_params=pltpu.CompilerParams(
            dimension_semantics=("arbitrary",), vmem_limit_bytes=VMEM_LIMIT_BYTES),
        name="hybrid_layer",
    )

    outs = []
    for bi in range(batch):
        xb = x[bi]
        for l in range(depth):
            sink_rows = jnp.repeat(attn_sinks[l].reshape(N_KV_HEADS, Q_PER_KV), WINDOW, axis=1)
            xb = call(
                xb, xb,
                mix_pre_norm[l].reshape(1, D_MODEL), mix_post_norm[l].reshape(1, D_MODEL),
                ffn_pre_norm[l].reshape(1, D_MODEL), ffn_post_norm[l].reshape(1, D_MODEL),
                sink_rows,
                w_in[l], w_out[l], w_up[l], w_down[l],
                conv_w[l], conv_b[l].reshape(1, 2 * D_FF),
                tabs["cos_base"], tabs["sin_base"], tabs["cos_intra"], tabs["sin_intra"],
                tabs["decay_t"], tabs["xi"], tabs["zeta"],
            )
        outs.append(xb)
    return outs[0][None] if batch == 1 else jnp.stack(outs, axis=0)
```

```python
import functools

import numpy as np
import jax
import jax.numpy as jnp
from jax import lax
from jax.experimental import pallas as pl
from jax.experimental.pallas import tpu as pltpu

D_MODEL = 1024
HEAD_DIM = 64
ATTN_W = 512
N_ATTN_HEADS = 8
N_KV_HEADS = 2
Q_PER_KV = N_ATTN_HEADS // N_KV_HEADS
KV_W = N_KV_HEADS * HEAD_DIM
WINDOW = 128
RET_W = 512
N_RET_HEADS = 4
RET_HEAD_DIM = 128
D_FF = 2816
RMS_EPS = 1e-6
GN_EPS = 1e-6
MASK_VALUE = -1e30
GELU_C = 0.7978845608028654

LANES = 128
SUBLANES = 8
MXU_TILE = 256
SEQ_BLOCK = 256
FF_TILE = MXU_TILE
N_FF_TILES = D_FF // FF_TILE
CONV_ROW0 = 2 * SUBLANES
STAGE_ROWS = SEQ_BLOCK
V7X_VMEM_BYTES = 64 * 1024 * 1024
VMEM_LIMIT_BYTES = V7X_VMEM_BYTES - 4 * 1024 * 1024

T_QA, T_KA, T_VA, T_VR, T_GR = 0, ATTN_W, ATTN_W + KV_W, ATTN_W + 2 * KV_W, ATTN_W + 2 * KV_W + RET_W
T_ROWS = ATTN_W + 2 * KV_W + 2 * RET_W
S_QR, S_KR = 0, RET_W
S_COLS = 2 * RET_W


def _retention_tables(seq):
    c = SEQ_BLOCK
    n_steps = seq // c
    angle = 1.0 / np.power(10000.0, np.linspace(0.0, 1.0, RET_HEAD_DIM // 2))
    angle = np.repeat(angle, 2)
    base = (np.arange(n_steps, dtype=np.float64) * c)[:, None] * angle[None]
    intra = np.arange(c, dtype=np.float64)[:, None] * angle[None]
    gamma = 1.0 - np.power(2.0, -5.0 - np.arange(N_RET_HEADS, dtype=np.float64))
    k_scale = RET_HEAD_DIM ** -0.5
    idx = np.arange(c, dtype=np.float64)
    rel = idx[None, :] - idx[:, None]
    decay_t = np.where(rel[None] >= 0, gamma[:, None, None] ** np.maximum(rel, 0.0)[None], 0.0)
    xi = gamma[:, None] ** (idx[None, :] + 1.0)
    zeta = gamma[:, None] ** (c - 1.0 - idx[None, :])
    f32 = lambda a: jnp.asarray(a, dtype=jnp.float32)
    return dict(
        cos_base=f32(np.cos(base)), sin_base=f32(np.sin(base)),
        cos_intra=f32(np.cos(intra)), sin_intra=f32(np.sin(intra)),
        decay_t=f32(k_scale * decay_t),
        xi=f32(np.broadcast_to(xi[:, :, None], (N_RET_HEADS, c, LANES))),
        zeta=f32(np.broadcast_to(k_scale * zeta[:, :, None], (N_RET_HEADS, c, LANES))),
        chunk_decay=[float(g ** c) for g in gamma],
    )


def _rms(x, w):
    return x * lax.rsqrt(jnp.mean(x * x, axis=-1, keepdims=True) + RMS_EPS) * w


def _dot(a, b):
    return jnp.dot(a, b, preferred_element_type=jnp.float32)


def _dot_nt(a, b):
    return lax.dot_general(a, b, (((1,), (1,)), ((), ())), preferred_element_type=jnp.float32)


def _dot_tn(a, b):
    return lax.dot_general(a, b, (((0,), (0,)), ((), ())), preferred_element_type=jnp.float32)


def _interleave(primary, filler):
    order, used = [], 0
    for i, unit in enumerate(primary):
        order.append(unit)
        upto = (i + 1) * len(filler) // len(primary)
        order.extend(filler[used:upto])
        used = upto
    return order


def _stage_weights(w_in_hbm, w_out_hbm, w_up_hbm, w_down_hbm,
                   w_t_ref, w_s_ref, w_out_ref, w_up_ref, w_down_ref, stage_ref, sem):
    bf16 = jnp.bfloat16
    R = STAGE_ROWS

    def plain(dst_ref, dst_col, scale=None):
        def store(tile, row0):
            dst_ref[row0:row0 + R, dst_col:dst_col + LANES] = (tile if scale is None else tile * scale).astype(bf16)
        return store

    def transposed(dst_row, scale=None):
        def store(tile, row0):
            tile = tile if scale is None else tile * scale
            w_t_ref[dst_row:dst_row + LANES, row0:row0 + R] = jnp.transpose(tile).astype(bf16)
        return store

    in_slabs = ([transposed(T_QA + i * LANES, HEAD_DIM ** -0.5) for i in range(ATTN_W // LANES)]
                + [transposed(T_KA), transposed(T_VA)]
                + [plain(w_s_ref, S_QR + i * LANES) for i in range(RET_W // LANES)]
                + [plain(w_s_ref, S_KR + i * LANES) for i in range(RET_W // LANES)]
                + [transposed(T_VR + i * LANES) for i in range(RET_W // LANES)]
                + [transposed(T_GR + i * LANES) for i in range(RET_W // LANES)])
    tiles = []
    for src, stores in ((w_in_hbm, in_slabs),
                        (w_out_hbm, [plain(w_out_ref, c) for c in range(0, D_MODEL, LANES)]),
                        (w_up_hbm, [plain(w_up_ref, c) for c in range(0, 2 * D_FF, LANES)]),
                        (w_down_hbm, [plain(w_down_ref, c, 0.5) for c in range(0, D_MODEL, LANES)])):
        for row0 in range(0, src.shape[0], R):
            tiles += [(src, row0, i * LANES, store) for i, store in enumerate(stores)]
    n_slots = stage_ref.shape[0]
    waves = [tiles[i:i + n_slots] for i in range(0, len(tiles), n_slots)]

    def copy(w, slot):
        src, row0, col0, _ = waves[w][slot]
        return pltpu.make_async_copy(src.at[pl.ds(row0, R), pl.ds(col0, LANES)],
                                     stage_ref.at[slot, pl.ds((w % 2) * R, R), :], sem.at[w % 2])

    def start(w):
        for slot in range(len(waves[w])):
            copy(w, slot).start()

    for w in range(min(2, len(waves))):
        start(w)
    for w in range(len(waves)):
        for slot in range(len(waves[w])):
            copy(w, slot).wait()
        for slot, (_, row0, _, store) in enumerate(waves[w]):
            store(stage_ref[slot, (w % 2) * R:(w % 2 + 1) * R, :], row0)
        if w + 2 < len(waves):
            start(w + 2)


def _layer_kernel(chunk_decay, n_blocks,
                  x_ref, xn_ref, g1_ref, g2_ref, g3_ref, g4_ref, sink_ref,
                  w_in_hbm, w_out_hbm, w_up_hbm, w_down_hbm, cw_ref, cb_ref,
                  cosb_ref, sinb_ref, cosi_ref, sini_ref, decay_ref, xi_ref, zeta_ref,
                  o_ref,
                  w_t_ref, w_s_ref, w_out_ref, w_up_ref, w_down_ref, stage_sem,
                  kprev_ref, vtprev_ref, state_ref, pbuf_ref, mixt_ref, y_ref, x1_ref, h_ref):
    step = pl.program_id(0)
    bf16 = jnp.bfloat16
    T = SEQ_BLOCK
    W = WINDOW

    @pl.when(step == 0)
    def _init():
        _stage_weights(w_in_hbm, w_out_hbm, w_up_hbm, w_down_hbm,
                       w_t_ref, w_s_ref, w_out_ref, w_up_ref, w_down_ref, pbuf_ref, stage_sem)
        kprev_ref[...] = jnp.zeros_like(kprev_ref)
        vtprev_ref[...] = jnp.zeros_like(vtprev_ref)
        state_ref[...] = jnp.zeros_like(state_ref)
        pbuf_ref[...] = jnp.zeros_like(pbuf_ref)
        x1_ref[...] = jnp.zeros_like(x1_ref)
        h_ref[...] = _rms(x_ref[...], g1_ref[...]).astype(bf16)

    blk = jnp.minimum(step, n_blocks - 1)

    h = h_ref[...]
    proj_t = _dot_nt(w_t_ref[...], h)
    proj_s = _dot(h, w_s_ref[...])
    h2 = _rms(x1_ref[...], g3_ref[...]).astype(bf16)

    def conv(p, slab, c0):
        pbuf_ref[slab, pl.ds(CONV_ROW0, T, stride=2), :] = p
        p1 = pbuf_ref[slab, pl.ds(CONV_ROW0 - 2, T, stride=2), :]
        p2 = pbuf_ref[slab, pl.ds(CONV_ROW0 - 4, T, stride=2), :]
        pbuf_ref[slab, CONV_ROW0 - 4:CONV_ROW0, :] = pbuf_ref[slab, CONV_ROW0 + 2 * T - 4:CONV_ROW0 + 2 * T, :]
        return (cw_ref[2:3, c0:c0 + LANES] * p + cw_ref[1:2, c0:c0 + LANES] * p1
                + cw_ref[0:1, c0:c0 + LANES] * p2 + cb_ref[:, c0:c0 + LANES])

    def ffn_up_unit(j):
        c_gate, c_val = j * FF_TILE, D_FF + j * FF_TILE
        p_gate = _dot(h2, w_up_ref[:, c_gate:c_gate + FF_TILE])
        p_val = _dot(h2, w_up_ref[:, c_val:c_val + FF_TILE])
        for half in range(FF_TILE // LANES):
            l0 = half * LANES
            ug = conv(p_gate[:, l0:l0 + LANES], (c_gate + l0) // LANES, c_gate + l0)
            uv = conv(p_val[:, l0:l0 + LANES], (c_val + l0) // LANES, c_val + l0)
            t = jnp.tanh(ug * (ug * ug * (GELU_C * 0.044715) + GELU_C))
            y_ref[:, c_gate + l0:c_gate + l0 + LANES] = (ug * (1.0 + t) * uv).astype(bf16)

    def ffn_down_unit(n):
        c0 = n * MXU_TILE
        o_ref[:, c0:c0 + MXU_TILE] = _dot(y_ref[...], w_down_ref[:, c0:c0 + MXU_TILE])

    for j in range(N_FF_TILES):
        ffn_up_unit(j)

    key_idx = lax.broadcasted_iota(jnp.int32, (W, Q_PER_KV * W), 0)
    qry_idx = lax.broadcasted_iota(jnp.int32, (W, Q_PER_KV * W), 1) & (W - 1)
    from_prev = key_idx > qry_idx
    first_block_penalty = jnp.where(step == 0, MASK_VALUE, 0.0)
    k_a = jnp.transpose(proj_t[T_KA:T_KA + KV_W, :])
    vt_a = proj_t[T_VA:T_VA + KV_W, :]
    zeros_half = jnp.zeros((HEAD_DIM, Q_PER_KV * W), bf16)
    kk_first = jnp.concatenate([kprev_ref[...], k_a[:W]], axis=0)
    vvt_first = jnp.concatenate([vtprev_ref[...], vt_a[:, :W]], axis=1)
    kprev_ref[...] = k_a[T - W:]
    vtprev_ref[...] = vt_a[:, T - W:]

    def attn_front(b, hk):
        t0 = b * W
        kk = (kk_first if b == 0 else k_a[t0 - W:t0 + W]).astype(bf16)
        h0 = hk * Q_PER_KV
        qt = jnp.concatenate(
            [proj_t[T_QA + (h0 + g) * HEAD_DIM:T_QA + (h0 + g + 1) * HEAD_DIM, t0:t0 + W]
             for g in range(Q_PER_KV)], axis=1).astype(bf16)
        rhs = jnp.concatenate([qt, zeros_half] if hk == 0 else [zeros_half, qt], axis=0)
        return _dot(kk, rhs)

    def attn_back(b, hk, st):
        t0 = b * W
        vvt = (vvt_first if b == 0 else vt_a[:, t0 - W:t0 + W]).astype(bf16)
        h0 = hk * Q_PER_KV
        s_prev = st[:W] + first_block_penalty if b == 0 else st[:W]
        s = jnp.where(from_prev, s_prev, st[W:])
        sink = sink_ref[hk:hk + 1, :]
        m = jnp.maximum(jnp.max(s, axis=0, keepdims=True), sink)
        p = jnp.exp(s - m)
        denom = jnp.sum(p, axis=0, keepdims=True) + jnp.exp(sink - m)
        pt = jnp.concatenate([jnp.where(from_prev, p, 0.0).astype(bf16),
                              jnp.where(from_prev, 0.0, p).astype(bf16)], axis=0)
        out_t = _dot(vvt[hk * HEAD_DIM:(hk + 1) * HEAD_DIM, :], pt) * (1.0 / denom)
        for g in range(Q_PER_KV):
            r0 = (h0 + g) * HEAD_DIM
            mixt_ref[r0:r0 + HEAD_DIM, t0:t0 + W] = out_t[:, g * W:(g + 1) * W].astype(bf16)

    lane = lax.broadcasted_iota(jnp.int32, (1, LANES), 1)
    lane_even = (lane & 1) == 0
    cos_b = cosb_ref[pl.ds(blk, 1), :]
    sin_b = sinb_ref[pl.ds(blk, 1), :]
    cos_i, sin_i = cosi_ref[...], sini_ref[...]
    cos = cos_b * cos_i - sin_b * sin_i
    sin = sin_b * cos_i + cos_b * sin_i
    sin_signed = jnp.where(lane_even, -sin, sin)

    def rotate(t):
        swapped = jnp.where(lane_even, pltpu.roll(t, LANES - 1, 1), pltpu.roll(t, 1, 1))
        return t * cos + swapped * sin_signed

    def ret_front(hh):
        c0 = hh * RET_HEAD_DIM
        q = rotate(proj_s[:, S_QR + c0:S_QR + c0 + RET_HEAD_DIM])
        k = rotate(proj_s[:, S_KR + c0:S_KR + c0 + RET_HEAD_DIM])
        return q, k, _dot_nt(k.astype(bf16), q.astype(bf16))

    def ret_back(hh, q, k, kq_t):
        c0 = hh * RET_HEAD_DIM
        vt = proj_t[T_VR + c0:T_VR + c0 + RET_HEAD_DIM, :].astype(bf16)
        gate_t = proj_t[T_GR + c0:T_GR + c0 + RET_HEAD_DIM, :]
        inner_t = kq_t * decay_ref[hh]
        state_t = state_ref[hh]
        o_t = (_dot(vt, inner_t.astype(bf16))
               + _dot_nt(state_t.astype(bf16), (q * xi_ref[hh]).astype(bf16)))
        state_ref[hh] = chunk_decay[hh] * state_t + _dot(vt, (k * zeta_ref[hh]).astype(bf16))
        mu = jnp.mean(o_t, axis=0, keepdims=True)
        d = o_t - mu
        var = jnp.mean(d * d, axis=0, keepdims=True)
        res = d * lax.rsqrt(var + GN_EPS) * (gate_t * (1.0 / (1.0 + jnp.exp(-gate_t))))
        mixt_ref[ATTN_W + c0:ATTN_W + c0 + RET_HEAD_DIM, :] = res.astype(bf16)

    def ffn_epilogue():
        o_ref[...] = x1_ref[...] + _rms(o_ref[...], g4_ref[...])

    def next_input_norm():
        h_ref[...] = _rms(xn_ref[...], g1_ref[...]).astype(bf16)

    attn_ids = [(b, hk) for b in range(T // W) for hk in range(N_KV_HEADS)]
    pairs = ([(attn_front, attn_back, ids) for ids in (attn_ids[0:2], attn_ids[2:4])]
             + [(ret_front, ret_back, ids) for ids in ([(0,), (1,)], [(2,), (3,)])])
    for n, (front, back, ids) in enumerate(pairs):
        fronts = [front(*i) for i in ids]
        ffn_down_unit(n)
        for i, f in zip(ids, fronts):
            back(*i, *(f if isinstance(f, tuple) else (f,)))
        if n == len(pairs) - 2:
            next_input_norm()
    ffn_epilogue()

    mixed = _dot_tn(mixt_ref[...], w_out_ref[...])
    x1_ref[...] = x_ref[...] + _rms(mixed, g2_ref[...])


def kernel(x, mix_pre_norm, w_in, attn_sinks, w_out, mix_post_norm, ffn_pre_norm, w_up,
           conv_w, conv_b, w_down, ffn_post_norm):
    batch, seq, d_model = x.shape
    depth = w_in.shape[0]
    assert d_model == D_MODEL and seq % SEQ_BLOCK == 0
    tabs = _retention_tables(seq)
    n_steps = seq // SEQ_BLOCK
    bf16 = jnp.bfloat16

    def resident(shape):
        nd = len(shape)
        return pl.BlockSpec(shape, lambda i, _n=nd: (0,) * _n, pipeline_mode=pl.Buffered(1))

    in_specs = [
        pl.BlockSpec((SEQ_BLOCK, D_MODEL), lambda i: (jnp.minimum(i, n_steps - 1), 0)),
        pl.BlockSpec((SEQ_BLOCK, D_MODEL), lambda i: (jnp.minimum(i + 1, n_steps - 1), 0)),
        resident((1, D_MODEL)), resident((1, D_MODEL)), resident((1, D_MODEL)), resident((1, D_MODEL)),
        resident((N_KV_HEADS, Q_PER_KV * WINDOW)),
        pl.BlockSpec(memory_space=pl.ANY), pl.BlockSpec(memory_space=pl.ANY),
        pl.BlockSpec(memory_space=pl.ANY), pl.BlockSpec(memory_space=pl.ANY),
        resident((3, 2 * D_FF)), resident((1, 2 * D_FF)),
        resident((n_steps, LANES)), resident((n_steps, LANES)),
        resident((SEQ_BLOCK, LANES)), resident((SEQ_BLOCK, LANES)),
        resident((N_RET_HEADS, SEQ_BLOCK, SEQ_BLOCK)),
        resident((N_RET_HEADS, SEQ_BLOCK, LANES)), resident((N_RET_HEADS, SEQ_BLOCK, LANES)),
    ]
    call = pl.pallas_call(
        functools.partial(_layer_kernel, tabs["chunk_decay"], n_steps),
        grid=(n_steps + 1,),
        in_specs=in_specs,
        out_specs=pl.BlockSpec((SEQ_BLOCK, D_MODEL), lambda i: (jnp.maximum(i - 1, 0), 0)),
        out_shape=jax.ShapeDtypeStruct((seq, D_MODEL), jnp.float32),
        scratch_shapes=[
            pltpu.VMEM((T_ROWS, D_MODEL), bf16),
            pltpu.VMEM((D_MODEL, S_COLS), bf16),
            pltpu.VMEM((D_MODEL, D_MODEL), bf16),
            pltpu.VMEM((D_MODEL, 2 * D_FF), bf16),
            pltpu.VMEM((D_FF, D_MODEL), bf16),
            pltpu.SemaphoreType.DMA((2,)),
            pltpu.VMEM((WINDOW, KV_W), jnp.float32),
            pltpu.VMEM((KV_W, WINDOW), jnp.float32),
            pltpu.VMEM((N_RET_HEADS, RET_HEAD_DIM, RET_HEAD_DIM), jnp.float32),
            pltpu.VMEM((2 * D_FF // LANES, CONV_ROW0 + 2 * SEQ_BLOCK, LANES), jnp.float32),
            pltpu.VMEM((D_MODEL, SEQ_BLOCK), bf16),
            pltpu.VMEM((SEQ_BLOCK, D_FF), bf16),
            pltpu.VMEM((SEQ_BLOCK, D_MODEL), jnp.float32),
            pltpu.VMEM((SEQ_BLOCK, D_MODEL), bf16),
        ],
        compiler_params=pltpu.CompilerParams(
            dimension_semantics=("arbitrary",), vmem_limit_bytes=VMEM_LIMIT_BYTES),
        name="hybrid_layer",
    )

    outs = []
    for bi in range(batch):
        xb = x[bi]
        for l in range(depth):
            sink_rows = jnp.repeat(attn_sinks[l].reshape(N_KV_HEADS, Q_PER_KV), WINDOW, axis=1)
            xb = call(
                xb, xb,
                mix_pre_norm[l].reshape(1, D_MODEL), mix_post_norm[l].reshape(1, D_MODEL),
                ffn_pre_norm[l].reshape(1, D_MODEL), ffn_post_norm[l].reshape(1, D_MODEL),
                sink_rows,
                w_in[l], w_out[l], w_up[l], w_down[l],
                conv_w[l], conv_b[l].reshape(1, 2 * D_FF),
                tabs["cos_base"], tabs["sin_base"], tabs["cos_intra"], tabs["sin_intra"],
                tabs["decay_t"], tabs["xi"], tabs["zeta"],
            )
        outs.append(xb)
    return outs[0][None] if batch == 1 else jnp.stack(outs, axis=0)
```

```python
import functools

import numpy as np
import jax
import jax.numpy as jnp
from jax import lax
from jax.experimental import pallas as pl
from jax.experimental.pallas import tpu as pltpu

D_MODEL = 1024
HEAD_DIM = 64
ATTN_W = 512
N_ATTN_HEADS = 8
N_KV_HEADS = 2
Q_PER_KV = N_ATTN_HEADS // N_KV_HEADS
KV_W = N_KV_HEADS * HEAD_DIM
WINDOW = 128
RET_W = 512
N_RET_HEADS = 4
RET_HEAD_DIM = 128
D_FF = 2816
RMS_EPS = 1e-6
GN_EPS = 1e-6
MASK_VALUE = -1e30
GELU_C = 0.7978845608028654

LANES = 128
SUBLANES = 8
MXU_TILE = 256
SEQ_BLOCK = 256
FF_TILE = MXU_TILE
N_FF_TILES = D_FF // FF_TILE
CONV_ROW0 = 2 * SUBLANES
STAGE_ROWS = SEQ_BLOCK
UP_STAGE_ROWS = 16
V7X_VMEM_BYTES = 64 * 1024 * 1024
VMEM_LIMIT_BYTES = V7X_VMEM_BYTES - 4 * 1024 * 1024

T_QA, T_KA, T_VA, T_VR, T_GR = 0, ATTN_W, ATTN_W + KV_W, ATTN_W + 2 * KV_W, ATTN_W + 2 * KV_W + RET_W
T_ROWS = ATTN_W + 2 * KV_W + 2 * RET_W
S_QR, S_KR = 0, RET_W
S_COLS = 2 * RET_W


def _retention_tables(seq):
    c = SEQ_BLOCK
    n_steps = seq // c
    angle = 1.0 / np.power(10000.0, np.linspace(0.0, 1.0, RET_HEAD_DIM // 2))
    angle = np.repeat(angle, 2)
    base = (np.arange(n_steps, dtype=np.float64) * c)[:, None] * angle[None]
    intra = np.arange(c, dtype=np.float64)[:, None] * angle[None]
    gamma = 1.0 - np.power(2.0, -5.0 - np.arange(N_RET_HEADS, dtype=np.float64))
    k_scale = RET_HEAD_DIM ** -0.5
    idx = np.arange(c, dtype=np.float64)
    rel = idx[None, :] - idx[:, None]
    decay_t = np.where(rel[None] >= 0, gamma[:, None, None] ** np.maximum(rel, 0.0)[None], 0.0)
    xi = gamma[:, None] ** (idx[None, :] + 1.0)
    zeta = gamma[:, None] ** (c - 1.0 - idx[None, :])
    f32 = lambda a: jnp.asarray(a, dtype=jnp.float32)
    return dict(
        cos_base=f32(np.cos(base)), sin_base=f32(np.sin(base)),
        cos_intra=f32(np.cos(intra)), sin_intra=f32(np.sin(intra)),
        decay_t=f32(k_scale * decay_t),
        xi=f32(np.broadcast_to(xi[:, :, None], (N_RET_HEADS, c, LANES))),
        zeta=f32(np.broadcast_to(k_scale * zeta[:, :, None], (N_RET_HEADS, c, LANES))),
        chunk_decay=[float(g ** c) for g in gamma],
    )


def _rms(x, w):
    return x * lax.rsqrt(jnp.mean(x * x, axis=-1, keepdims=True) + RMS_EPS) * w


def _dot(a, b):
    return jnp.dot(a, b, preferred_element_type=jnp.float32)


def _dot_nt(a, b):
    return lax.dot_general(a, b, (((1,), (1,)), ((), ())), preferred_element_type=jnp.float32)


def _dot_tn(a, b):
    return lax.dot_general(a, b, (((0,), (0,)), ((), ())), preferred_element_type=jnp.float32)


def _interleave(primary, filler):
    order, used = [], 0
    for i, unit in enumerate(primary):
        order.append(unit)
        upto = (i + 1) * len(filler) // len(primary)
        order.extend(filler[used:upto])
        used = upto
    return order


def _stage_rows(src_hbm, dst_ref, stage_halves, sem, scale=None):
    rows = stage_halves[0].shape[0]
    n_chunks = src_hbm.shape[0] // rows

    def copy(i):
        return pltpu.make_async_copy(src_hbm.at[pl.ds(i * rows, rows), :], stage_halves[i % 2], sem.at[i % 2])

    for i in range(min(2, n_chunks)):
        copy(i).start()
    for i in range(n_chunks):
        copy(i).wait()
        tile = stage_halves[i % 2][...]
        dst_ref[i * rows:(i + 1) * rows, :] = (tile if scale is None else tile * scale).astype(jnp.bfloat16)
        if i + 2 < n_chunks:
            copy(i + 2).start()


def _stage_input_projection(w_in_hbm, w_t_ref, w_s_ref, stage_ref, sem):
    bf16 = jnp.bfloat16
    R = STAGE_ROWS

    def plain(dst_ref, dst_col, scale=None):
        def store(tile, row0):
            dst_ref[row0:row0 + R, dst_col:dst_col + LANES] = (tile if scale is None else tile * scale).astype(bf16)
        return store

    def transposed(dst_row, scale=None):
        def store(tile, row0):
            tile = tile if scale is None else tile * scale
            w_t_ref[dst_row:dst_row + LANES, row0:row0 + R] = jnp.transpose(tile).astype(bf16)
        return store

    in_slabs = ([transposed(T_QA + i * LANES, HEAD_DIM ** -0.5) for i in range(ATTN_W // LANES)]
                + [transposed(T_KA), transposed(T_VA)]
                + [plain(w_s_ref, S_QR + i * LANES) for i in range(RET_W // LANES)]
                + [plain(w_s_ref, S_KR + i * LANES) for i in range(RET_W // LANES)]
                + [transposed(T_VR + i * LANES) for i in range(RET_W // LANES)]
                + [transposed(T_GR + i * LANES) for i in range(RET_W // LANES)])
    tiles = []
    for row0 in range(0, w_in_hbm.shape[0], R):
        tiles += [(w_in_hbm, row0, i * LANES, store) for i, store in enumerate(in_slabs)]
    n_slots = stage_ref.shape[0]
    waves = [tiles[i:i + n_slots] for i in range(0, len(tiles), n_slots)]

    def copy(w, slot):
        src, row0, col0, _ = waves[w][slot]
        return pltpu.make_async_copy(src.at[pl.ds(row0, R), pl.ds(col0, LANES)],
                                     stage_ref.at[slot, pl.ds((w % 2) * R, R), :], sem.at[w % 2])

    def start(w):
        for slot in range(len(waves[w])):
            copy(w, slot).start()

    for w in range(min(2, len(waves))):
        start(w)
    for w in range(len(waves)):
        for slot in range(len(waves[w])):
            copy(w, slot).wait()
        for slot, (_, row0, _, store) in enumerate(waves[w]):
            store(stage_ref[slot, (w % 2) * R:(w % 2 + 1) * R, :], row0)
        if w + 2 < len(waves):
            start(w + 2)


def _layer_kernel(chunk_decay, n_blocks,
                  xp_ref, xn_ref, g1_ref, g2_ref, g3_ref, g4_ref, sink_ref,
                  w_in_hbm, w_out_hbm, w_up_hbm, w_down_hbm, cw_ref, cb_ref,
                  cosb_ref, sinb_ref, cosi_ref, sini_ref, decay_ref, xi_ref, zeta_ref,
                  o_ref,
                  w_t_ref, w_s_ref, w_out_ref, w_up_ref, w_down_ref, up_stage_ref, stage_sem,
                  kprev_ref, vtprev_ref, state_ref, pbuf_ref, mixt_ref, y_ref, x1_ref, h_ref):
    step = pl.program_id(0)
    bf16 = jnp.bfloat16
    T = SEQ_BLOCK
    W = WINDOW

    @pl.when(step == 0)
    def _init():
        _stage_input_projection(w_in_hbm, w_t_ref, w_s_ref, pbuf_ref, stage_sem)
        x1_halves = [x1_ref.at[pl.ds(i * (T // 2), T // 2), :] for i in range(2)]
        _stage_rows(w_out_hbm, w_out_ref, x1_halves, stage_sem)
        _stage_rows(w_down_hbm, w_down_ref, x1_halves, stage_sem, scale=0.5)
        _stage_rows(w_up_hbm, w_up_ref, [up_stage_ref.at[i] for i in range(2)], stage_sem)
        kprev_ref[...] = jnp.zeros_like(kprev_ref)
        vtprev_ref[...] = jnp.zeros_like(vtprev_ref)
        state_ref[...] = jnp.zeros_like(state_ref)
        pbuf_ref[...] = jnp.zeros_like(pbuf_ref)
        x1_ref[...] = jnp.zeros_like(x1_ref)
        h_ref[...] = _rms(xp_ref[...], g1_ref[...]).astype(bf16)

    blk = jnp.minimum(step, n_blocks - 1)
    prev_valid = jnp.where(step > 0, 1.0, 0.0)

    h = h_ref[...]
    proj_t = _dot_nt(w_t_ref[...], h)
    proj_s = _dot(h, w_s_ref[...])
    x1 = (xp_ref[...] + _rms(x1_ref[...], g2_ref[...])) * prev_valid
    x1_ref[...] = x1
    h2 = _rms(x1, g3_ref[...]).astype(bf16)

    def conv(p, slab, c0):
        pbuf_ref[slab, pl.ds(CONV_ROW0, T, stride=2), :] = p
        p1 = pbuf_ref[slab, pl.ds(CONV_ROW0 - 2, T, stride=2), :]
        p2 = pbuf_ref[slab, pl.ds(CONV_ROW0 - 4, T, stride=2), :]
        pbuf_ref[slab, CONV_ROW0 - 4:CONV_ROW0, :] = pbuf_ref[slab, CONV_ROW0 + 2 * T - 4:CONV_ROW0 + 2 * T, :]
        return (cw_ref[2:3, c0:c0 + LANES] * p + cw_ref[1:2, c0:c0 + LANES] * p1
                + cw_ref[0:1, c0:c0 + LANES] * p2 + cb_ref[:, c0:c0 + LANES])

    def ffn_up_unit(j):
        c_gate, c_val = j * FF_TILE, D_FF + j * FF_TILE
        p_gate = _dot(h2, w_up_ref[:, c_gate:c_gate + FF_TILE])
        p_val = _dot(h2, w_up_ref[:, c_val:c_val + FF_TILE])
        for half in range(FF_TILE // LANES):
            l0 = half * LANES
            ug = conv(p_gate[:, l0:l0 + LANES], (c_gate + l0) // LANES, c_gate + l0)
            uv = conv(p_val[:, l0:l0 + LANES], (c_val + l0) // LANES, c_val + l0)
            t = jnp.tanh(ug * (ug * ug * (GELU_C * 0.044715) + GELU_C))
            y_ref[:, c_gate + l0:c_gate + l0 + LANES] = (ug * (1.0 + t) * uv).astype(bf16)

    def ffn_down_unit(n):
        c0 = n * MXU_TILE
        o_ref[:, c0:c0 + MXU_TILE] = _dot(y_ref[...], w_down_ref[:, c0:c0 + MXU_TILE])

    for j in range(N_FF_TILES):
        ffn_up_unit(j)

    key_idx = lax.broadcasted_iota(jnp.int32, (W, Q_PER_KV * W), 0)
    qry_idx = lax.broadcasted_iota(jnp.int32, (W, Q_PER_KV * W), 1) & (W - 1)
    from_prev = key_idx > qry_idx
    first_block_penalty = jnp.where(step == 0, MASK_VALUE, 0.0)
    k_a = jnp.transpose(proj_t[T_KA:T_KA + KV_W, :])
    vt_a = proj_t[T_VA:T_VA + KV_W, :]
    zeros_half = jnp.zeros((HEAD_DIM, Q_PER_KV * W), bf16)
    kk_first = jnp.concatenate([kprev_ref[...], k_a[:W]], axis=0)
    vvt_first = jnp.concatenate([vtprev_ref[...], vt_a[:, :W]], axis=1)
    kprev_ref[...] = k_a[T - W:]
    vtprev_ref[...] = vt_a[:, T - W:]

    def attn_front(b, hk):
        t0 = b * W
        kk = (kk_first if b == 0 else k_a[t0 - W:t0 + W]).astype(bf16)
        h0 = hk * Q_PER_KV
        qt = jnp.concatenate(
            [proj_t[T_QA + (h0 + g) * HEAD_DIM:T_QA + (h0 + g + 1) * HEAD_DIM, t0:t0 + W]
             for g in range(Q_PER_KV)], axis=1).astype(bf16)
        rhs = jnp.concatenate([qt, zeros_half] if hk == 0 else [zeros_half, qt], axis=0)
        return _dot(kk, rhs)

    def attn_back(b, hk, st):
        t0 = b * W
        vvt = (vvt_first if b == 0 else vt_a[:, t0 - W:t0 + W]).astype(bf16)
        h0 = hk * Q_PER_KV
        s_prev = st[:W] + first_block_penalty if b == 0 else st[:W]
        s = jnp.where(from_prev, s_prev, st[W:])
        sink = sink_ref[hk:hk + 1, :]
        m = jnp.maximum(jnp.max(s, axis=0, keepdims=True), sink)
        p = jnp.exp(s - m)
        denom = jnp.sum(p, axis=0, keepdims=True) + jnp.exp(sink - m)
        pt = jnp.concatenate([jnp.where(from_prev, p, 0.0).astype(bf16),
                              jnp.where(from_prev, 0.0, p).astype(bf16)], axis=0)
        out_t = _dot(vvt[hk * HEAD_DIM:(hk + 1) * HEAD_DIM, :], pt) * (1.0 / denom)
        for g in range(Q_PER_KV):
            r0 = (h0 + g) * HEAD_DIM
            mixt_ref[r0:r0 + HEAD_DIM, t0:t0 + W] = out_t[:, g * W:(g + 1) * W].astype(bf16)

    lane = lax.broadcasted_iota(jnp.int32, (1, LANES), 1)
    lane_even = (lane & 1) == 0
    cos_b = cosb_ref[pl.ds(blk, 1), :]
    sin_b = sinb_ref[pl.ds(blk, 1), :]
    cos_i, sin_i = cosi_ref[...], sini_ref[...]
    cos = cos_b * cos_i - sin_b * sin_i
    sin = sin_b * cos_i + cos_b * sin_i
    sin_signed = jnp.where(lane_even, -sin, sin)

    def rotate(t):
        swapped = jnp.where(lane_even, pltpu.roll(t, LANES - 1, 1), pltpu.roll(t, 1, 1))
        return t * cos + swapped * sin_signed

    def ret_front(hh):
        c0 = hh * RET_HEAD_DIM
        q = rotate(proj_s[:, S_QR + c0:S_QR + c0 + RET_HEAD_DIM])
        k = rotate(proj_s[:, S_KR + c0:S_KR + c0 + RET_HEAD_DIM])
        return q, k, _dot_nt(k.astype(bf16), q.astype(bf16))

    def ret_back(hh, q, k, kq_t):
        c0 = hh * RET_HEAD_DIM
        vt = proj_t[T_VR + c0:T_VR + c0 + RET_HEAD_DIM, :].astype(bf16)
        gate_t = proj_t[T_GR + c0:T_GR + c0 + RET_HEAD_DIM, :]
        inner_t = kq_t * decay_ref[hh]
        state_t = state_ref[hh]
        o_t = (_dot(vt, inner_t.astype(bf16))
               + _dot_nt(state_t.astype(bf16), (q * xi_ref[hh]).astype(bf16)))
        state_ref[hh] = chunk_decay[hh] * state_t + _dot(vt, (k * zeta_ref[hh]).astype(bf16))
        mu = jnp.mean(o_t, axis=0, keepdims=True)
        d = o_t - mu
        var = jnp.mean(d * d, axis=0, keepdims=True)
        res = d * lax.rsqrt(var + GN_EPS) * (gate_t * (1.0 / (1.0 + jnp.exp(-gate_t))))
        mixt_ref[ATTN_W + c0:ATTN_W + c0 + RET_HEAD_DIM, :] = res.astype(bf16)

    def ffn_epilogue():
        o_ref[...] = x1_ref[...] + _rms(o_ref[...], g4_ref[...])

    def next_input_norm():
        h_ref[...] = _rms(xn_ref[...], g1_ref[...]).astype(bf16)

    attn_ids = [(b, hk) for b in range(T // W) for hk in range(N_KV_HEADS)]
    pairs = ([(attn_front, attn_back, ids) for ids in (attn_ids[0:2], attn_ids[2:4])]
             + [(ret_front, ret_back, ids) for ids in ([(0,), (1,)], [(2,), (3,)])])
    for n, (front, back, ids) in enumerate(pairs):
        fronts = [front(*i) for i in ids]
        ffn_down_unit(n)
        for i, f in zip(ids, fronts):
            back(*i, *(f if isinstance(f, tuple) else (f,)))
        if n == len(pairs) - 2:
            next_input_norm()
    ffn_epilogue()

    x1_ref[...] = _dot_tn(mixt_ref[...], w_out_ref[...])


def kernel(x, mix_pre_norm, w_in, attn_sinks, w_out, mix_post_norm, ffn_pre_norm, w_up,
           conv_w, conv_b, w_down, ffn_post_norm):
    batch, seq, d_model = x.shape
    depth = w_in.shape[0]
    assert d_model == D_MODEL and seq % SEQ_BLOCK == 0
    tabs = _retention_tables(seq)
    n_steps = seq // SEQ_BLOCK
    bf16 = jnp.bfloat16

    def resident(shape):
        nd = len(shape)
        return pl.BlockSpec(shape, lambda i, _n=nd: (0,) * _n, pipeline_mode=pl.Buffered(1))

    in_specs = [
        pl.BlockSpec((SEQ_BLOCK, D_MODEL), lambda i: (jnp.maximum(i - 1, 0), 0)),
        pl.BlockSpec((SEQ_BLOCK, D_MODEL), lambda i: (jnp.minimum(i + 1, n_steps - 1), 0)),
        resident((1, D_MODEL)), resident((1, D_MODEL)), resident((1, D_MODEL)), resident((1, D_MODEL)),
        resident((N_KV_HEADS, Q_PER_KV * WINDOW)),
        pl.BlockSpec(memory_space=pl.ANY), pl.BlockSpec(memory_space=pl.ANY),
        pl.BlockSpec(memory_space=pl.ANY), pl.BlockSpec(memory_space=pl.ANY),
        resident((3, 2 * D_FF)), resident((1, 2 * D_FF)),
        resident((n_steps, LANES)), resident((n_steps, LANES)),
        resident((SEQ_BLOCK, LANES)), resident((SEQ_BLOCK, LANES)),
        resident((N_RET_HEADS, SEQ_BLOCK, SEQ_BLOCK)),
        resident((N_RET_HEADS, SEQ_BLOCK, LANES)), resident((N_RET_HEADS, SEQ_BLOCK, LANES)),
    ]
    call = pl.pallas_call(
        functools.partial(_layer_kernel, tabs["chunk_decay"], n_steps),
        grid=(n_steps + 1,),
        in_specs=in_specs,
        out_specs=pl.BlockSpec((SEQ_BLOCK, D_MODEL), lambda i: (jnp.maximum(i - 1, 0), 0)),
        out_shape=jax.ShapeDtypeStruct((seq, D_MODEL), jnp.float32),
        scratch_shapes=[
            pltpu.VMEM((T_ROWS, D_MODEL), bf16),
            pltpu.VMEM((D_MODEL, S_COLS), bf16),
            pltpu.VMEM((D_MODEL, D_MODEL), bf16),
            pltpu.VMEM((D_MODEL, 2 * D_FF), bf16),
            pltpu.VMEM((D_FF, D_MODEL), bf16),
            pltpu.VMEM((2, UP_STAGE_ROWS, 2 * D_FF), jnp.float32),
            pltpu.SemaphoreType.DMA((2,)),
            pltpu.VMEM((WINDOW, KV_W), jnp.float32),
            pltpu.VMEM((KV_W, WINDOW), jnp.float32),
            pltpu.VMEM((N_RET_HEADS, RET_HEAD_DIM, RET_HEAD_DIM), jnp.float32),
            pltpu.VMEM((2 * D_FF // LANES, CONV_ROW0 + 2 * SEQ_BLOCK, LANES), jnp.float32),
            pltpu.VMEM((D_MODEL, SEQ_BLOCK), bf16),
            pltpu.VMEM((SEQ_BLOCK, D_FF), bf16),
            pltpu.VMEM((SEQ_BLOCK, D_MODEL), jnp.float32),
            pltpu.VMEM((SEQ_BLOCK, D_MODEL), bf16),
        ],
        compiler_params=pltpu.CompilerParams(
            dimension_semantics=("arbitrary",), vmem_limit_bytes=VMEM_LIMIT_BYTES),
        name="hybrid_layer",
    )

    outs = []
    for bi in range(batch):
        xb = x[bi]
        for l in range(depth):
            sink_rows = jnp.repeat(attn_sinks[l].reshape(N_KV_HEADS, Q_PER_KV), WINDOW, axis=1)
            xb = call(
                xb, xb,
                mix_pre_norm[l].reshape(1, D_MODEL), mix_post_norm[l].reshape(1, D_MODEL),
                ffn_pre_norm[l].reshape(1, D_MODEL), ffn_post_norm[l].reshape(1, D_MODEL),
                sink_rows,
                w_in[l], w_out[l], w_up[l], w_down[l],
                conv_w[l], conv_b[l].reshape(1, 2 * D_FF),
                tabs["cos_base"], tabs["sin_base"], tabs["cos_intra"], tabs["sin_intra"],
                tabs["decay_t"], tabs["xi"], tabs["zeta"],
            )
        outs.append(xb)
    return outs[0][None] if batch == 1 else jnp.stack(outs, axis=0)
```

```python
import functools

import numpy as np
import jax
import jax.numpy as jnp
from jax import lax
from jax.experimental import pallas as pl
from jax.experimental.pallas import tpu as pltpu

D_MODEL = 1024
HEAD_DIM = 64
ATTN_W = 512
N_ATTN_HEADS = 8
N_KV_HEADS = 2
Q_PER_KV = N_ATTN_HEADS // N_KV_HEADS
KV_W = N_KV_HEADS * HEAD_DIM
WINDOW = 128
RET_W = 512
N_RET_HEADS = 4
RET_HEAD_DIM = 128
D_FF = 2816
RMS_EPS = 1e-6
GN_EPS = 1e-6
MASK_VALUE = -1e30
GELU_C = 0.7978845608028654

LANES = 128
SUBLANES = 8
MXU_TILE = 256
SEQ_BLOCK = 256
FF_TILE = MXU_TILE
N_FF_TILES = D_FF // FF_TILE
CONV_ROW0 = 2 * SUBLANES
STAGE_ROWS = SEQ_BLOCK
V7X_VMEM_BYTES = 64 * 1024 * 1024
VMEM_LIMIT_BYTES = V7X_VMEM_BYTES - 4 * 1024 * 1024

T_QA, T_KA, T_VA, T_VR, T_GR = 0, ATTN_W, ATTN_W + KV_W, ATTN_W + 2 * KV_W, ATTN_W + 2 * KV_W + RET_W
T_ROWS = ATTN_W + 2 * KV_W + 2 * RET_W
S_QR, S_KR = 0, RET_W
S_COLS = 2 * RET_W


def _retention_tables(seq):
    c = SEQ_BLOCK
    n_steps = seq // c
    angle = 1.0 / np.power(10000.0, np.linspace(0.0, 1.0, RET_HEAD_DIM // 2))
    angle = np.repeat(angle, 2)
    base = (np.arange(n_steps, dtype=np.float64) * c)[:, None] * angle[None]
    intra = np.arange(c, dtype=np.float64)[:, None] * angle[None]
    gamma = 1.0 - np.power(2.0, -5.0 - np.arange(N_RET_HEADS, dtype=np.float64))
    k_scale = RET_HEAD_DIM ** -0.5
    idx = np.arange(c, dtype=np.float64)
    rel = idx[None, :] - idx[:, None]
    decay_t = np.where(rel[None] >= 0, gamma[:, None, None] ** np.maximum(rel, 0.0)[None], 0.0)
    xi = gamma[:, None] ** (idx[None, :] + 1.0)
    zeta = gamma[:, None] ** (c - 1.0 - idx[None, :])
    f32 = lambda a: jnp.asarray(a, dtype=jnp.float32)
    return dict(
        cos_base=f32(np.cos(base)), sin_base=f32(np.sin(base)),
        cos_intra=f32(np.cos(intra)), sin_intra=f32(np.sin(intra)),
        decay_t=f32(k_scale * decay_t),
        xi=f32(np.broadcast_to(xi[:, :, None], (N_RET_HEADS, c, LANES))),
        zeta=f32(np.broadcast_to(k_scale * zeta[:, :, None], (N_RET_HEADS, c, LANES))),
        chunk_decay=[float(g ** c) for g in gamma],
    )


def _rms(x, w):
    return x * lax.rsqrt(jnp.mean(x * x, axis=-1, keepdims=True) + RMS_EPS) * w


def _dot(a, b):
    return jnp.dot(a, b, preferred_element_type=jnp.float32)


def _dot_nt(a, b):
    return lax.dot_general(a, b, (((1,), (1,)), ((), ())), preferred_element_type=jnp.float32)


def _dot_tn(a, b):
    return lax.dot_general(a, b, (((0,), (0,)), ((), ())), preferred_element_type=jnp.float32)


def _up_col(c):
    part, c = divmod(c, D_FF)
    return (c // FF_TILE) * 2 * FF_TILE + part * FF_TILE + c % FF_TILE


def _stage_weights(w_in_hbm, w_out_hbm, w_up_hbm, w_down_hbm,
                   w_t_ref, w_s_ref, w_out_ref, w_up_ref, w_down_ref, stage_ref, sem):
    bf16 = jnp.bfloat16
    R = STAGE_ROWS

    def plain(dst_ref, dst_col, scale=None):
        def store(tile, row0):
            dst_ref[row0:row0 + R, dst_col:dst_col + LANES] = (tile if scale is None else tile * scale).astype(bf16)
        return store

    def transposed(dst_row, scale=None):
        def store(tile, row0):
            tile = tile if scale is None else tile * scale
            w_t_ref[dst_row:dst_row + LANES, row0:row0 + R] = jnp.transpose(tile).astype(bf16)
        return store

    in_slabs = ([transposed(T_QA + i * LANES, HEAD_DIM ** -0.5) for i in range(ATTN_W // LANES)]
                + [transposed(T_KA), transposed(T_VA)]
                + [plain(w_s_ref, S_QR + i * LANES) for i in range(RET_W // LANES)]
                + [plain(w_s_ref, S_KR + i * LANES) for i in range(RET_W // LANES)]
                + [transposed(T_VR + i * LANES) for i in range(RET_W // LANES)]
                + [transposed(T_GR + i * LANES) for i in range(RET_W // LANES)])
    tiles = []
    for src, stores in ((w_in_hbm, in_slabs),
                        (w_out_hbm, [plain(w_out_ref, c) for c in range(0, D_MODEL, LANES)]),
                        (w_up_hbm, [plain(w_up_ref, _up_col(c)) for c in range(0, 2 * D_FF, LANES)]),
                        (w_down_hbm, [plain(w_down_ref, c, 0.5) for c in range(0, D_MODEL, LANES)])):
        for row0 in range(0, src.shape[0], R):
            tiles += [(src, row0, i * LANES, store) for i, store in enumerate(stores)]

    n_slabs = stage_ref.shape[0]
    n_slots = 2 * n_slabs

    def copy(t):
        src, row0, col0, _ = tiles[t]
        slot = t % n_slots
        return pltpu.make_async_copy(src.at[pl.ds(row0, R), pl.ds(col0, LANES)],
                                     stage_ref.at[slot % n_slabs, pl.ds((slot // n_slabs) * R, R), :],
                                     sem.at[slot])

    for t in range(min(n_slots, len(tiles))):
        copy(t).start()
    for t, (_, row0, _, store) in enumerate(tiles):
        copy(t).wait()
        slot = t % n_slots
        half = slot // n_slabs
        store(stage_ref[slot % n_slabs, half * R:(half + 1) * R, :], row0)
        if t + n_slots < len(tiles):
            copy(t + n_slots).start()


def _layer_kernel(chunk_decay, n_blocks,
                  x_ref, xn_ref, g1_ref, g2_ref, g3_ref, g4_ref, sink_ref,
                  w_in_hbm, w_out_hbm, w_up_hbm, w_down_hbm, cw_ref, cb_ref,
                  cosb_ref, sinb_ref, cosi_ref, sini_ref, decay_ref, xi_ref, zeta_ref,
                  o_ref,
                  w_t_ref, w_s_ref, w_out_ref, w_up_ref, w_down_ref, stage_sem,
                  kprev_ref, vtprev_ref, state_ref, pbuf_ref, mixt_ref, y_ref, x1_ref, h_ref):
    step = pl.program_id(0)
    bf16 = jnp.bfloat16
    T = SEQ_BLOCK
    W = WINDOW

    @pl.when(step == 0)
    def _init():
        _stage_weights(w_in_hbm, w_out_hbm, w_up_hbm, w_down_hbm,
                       w_t_ref, w_s_ref, w_out_ref, w_up_ref, w_down_ref, pbuf_ref, stage_sem)
        kprev_ref[...] = jnp.zeros_like(kprev_ref)
        vtprev_ref[...] = jnp.zeros_like(vtprev_ref)
        state_ref[...] = jnp.zeros_like(state_ref)
        pbuf_ref[...] = jnp.zeros_like(pbuf_ref)
        x1_ref[...] = jnp.zeros_like(x1_ref)
        h_ref[...] = _rms(x_ref[...], g1_ref[...]).astype(bf16)

    blk = jnp.minimum(step, n_blocks - 1)

    h = h_ref[...]
    proj_t = _dot_nt(w_t_ref[...], h)
    proj_s = _dot(h, w_s_ref[...])
    h2 = _rms(x1_ref[...], g3_ref[...]).astype(bf16)

    def conv(p, slab, c0):
        pbuf_ref[slab, pl.ds(CONV_ROW0, T, stride=2), :] = p
        p1 = pbuf_ref[slab, pl.ds(CONV_ROW0 - 2, T, stride=2), :]
        p2 = pbuf_ref[slab, pl.ds(CONV_ROW0 - 4, T, stride=2), :]
        pbuf_ref[slab, CONV_ROW0 - 4:CONV_ROW0, :] = pbuf_ref[slab, CONV_ROW0 + 2 * T - 4:CONV_ROW0 + 2 * T, :]
        return (cw_ref[2:3, c0:c0 + LANES] * p + cw_ref[1:2, c0:c0 + LANES] * p1
                + cw_ref[0:1, c0:c0 + LANES] * p2 + cb_ref[:, c0:c0 + LANES])

    def ffn_up_unit(j):
        c_gate, c_val = j * FF_TILE, D_FF + j * FF_TILE
        p = _dot(h2, w_up_ref[:, 2 * c_gate:2 * c_gate + 2 * FF_TILE])
        p_gate, p_val = p[:, :FF_TILE], p[:, FF_TILE:]
        for half in range(FF_TILE // LANES):
            l0 = half * LANES
            ug = conv(p_gate[:, l0:l0 + LANES], (c_gate + l0) // LANES, c_gate + l0)
            uv = conv(p_val[:, l0:l0 + LANES], (c_val + l0) // LANES, c_val + l0)
            t = jnp.tanh(ug * (ug * ug * (GELU_C * 0.044715) + GELU_C))
            y_ref[:, c_gate + l0:c_gate + l0 + LANES] = (ug * (1.0 + t) * uv).astype(bf16)

    def ffn_down_unit(n):
        c0 = n * MXU_TILE
        o_ref[:, c0:c0 + MXU_TILE] = _dot(y_ref[...], w_down_ref[:, c0:c0 + MXU_TILE])

    for j in range(N_FF_TILES):
        ffn_up_unit(j)

    key_idx = lax.broadcasted_iota(jnp.int32, (W, Q_PER_KV * W), 0)
    qry_idx = lax.broadcasted_iota(jnp.int32, (W, Q_PER_KV * W), 1) & (W - 1)
    from_prev = key_idx > qry_idx
    first_block_penalty = jnp.where(step == 0, MASK_VALUE, 0.0)
    k_a = jnp.transpose(proj_t[T_KA:T_KA + KV_W, :])
    vt_a = proj_t[T_VA:T_VA + KV_W, :]
    zeros_half = jnp.zeros((HEAD_DIM, Q_PER_KV * W), bf16)
    kk_first = jnp.concatenate([kprev_ref[...], k_a[:W]], axis=0)
    vvt_first = jnp.concatenate([vtprev_ref[...], vt_a[:, :W]], axis=1)
    kprev_ref[...] = k_a[T - W:]
    vtprev_ref[...] = vt_a[:, T - W:]

    def attn_front(b, hk):
        t0 = b * W
        kk = (kk_first if b == 0 else k_a[t0 - W:t0 + W]).astype(bf16)
        h0 = hk * Q_PER_KV
        qt = jnp.concatenate(
            [proj_t[T_QA + (h0 + g) * HEAD_DIM:T_QA + (h0 + g + 1) * HEAD_DIM, t0:t0 + W]
             for g in range(Q_PER_KV)], axis=1).astype(bf16)
        rhs = jnp.concatenate([qt, zeros_half] if hk == 0 else [zeros_half, qt], axis=0)
        return _dot(kk, rhs)

    def attn_back(b, hk, st):
        t0 = b * W
        vvt = (vvt_first if b == 0 else vt_a[:, t0 - W:t0 + W]).astype(bf16)
        h0 = hk * Q_PER_KV
        s_prev = st[:W] + first_block_penalty if b == 0 else st[:W]
        s = jnp.where(from_prev, s_prev, st[W:])
        sink = sink_ref[hk:hk + 1, :]
        m = jnp.maximum(jnp.max(s, axis=0, keepdims=True), sink)
        p = jnp.exp(s - m)
        denom = jnp.sum(p, axis=0, keepdims=True) + jnp.exp(sink - m)
        pt = jnp.concatenate([jnp.where(from_prev, p, 0.0).astype(bf16),
                              jnp.where(from_prev, 0.0, p).astype(bf16)], axis=0)
        out_t = _dot(vvt[hk * HEAD_DIM:(hk + 1) * HEAD_DIM, :], pt) * (1.0 / denom)
        for g in range(Q_PER_KV):
            r0 = (h0 + g) * HEAD_DIM
            mixt_ref[r0:r0 + HEAD_DIM, t0:t0 + W] = out_t[:, g * W:(g + 1) * W].astype(bf16)

    lane = lax.broadcasted_iota(jnp.int32, (1, LANES), 1)
    lane_even = (lane & 1) == 0
    cos_b = cosb_ref[pl.ds(blk, 1), :]
    sin_b = sinb_ref[pl.ds(blk, 1), :]
    cos_i, sin_i = cosi_ref[...], sini_ref[...]
    cos = cos_b * cos_i - sin_b * sin_i
    sin = sin_b * cos_i + cos_b * sin_i
    sin_signed = jnp.where(lane_even, -sin, sin)

    def rotate(t):
        swapped = jnp.where(lane_even, pltpu.roll(t, LANES - 1, 1), pltpu.roll(t, 1, 1))
        return t * cos + swapped * sin_signed

    def ret_front(hh):
        c0 = hh * RET_HEAD_DIM
        q = rotate(proj_s[:, S_QR + c0:S_QR + c0 + RET_HEAD_DIM])
        k = rotate(proj_s[:, S_KR + c0:S_KR + c0 + RET_HEAD_DIM])
        return q, k, _dot_nt(k.astype(bf16), q.astype(bf16))

    def ret_back(hh, q, k, kq_t):
        c0 = hh * RET_HEAD_DIM
        vt = proj_t[T_VR + c0:T_VR + c0 + RET_HEAD_DIM, :].astype(bf16)
        gate_t = proj_t[T_GR + c0:T_GR + c0 + RET_HEAD_DIM, :]
        inner_t = kq_t * decay_ref[hh]
        state_t = state_ref[hh]
        o_t = (_dot(vt, inner_t.astype(bf16))
               + _dot_nt(state_t.astype(bf16), (q * xi_ref[hh]).astype(bf16)))
        state_ref[hh] = chunk_decay[hh] * state_t + _dot(vt, (k * zeta_ref[hh]).astype(bf16))
        mu = jnp.mean(o_t, axis=0, keepdims=True)
        d = o_t - mu
        var = jnp.mean(d * d, axis=0, keepdims=True)
        res = d * lax.rsqrt(var + GN_EPS) * (gate_t * (1.0 / (1.0 + jnp.exp(-gate_t))))
        mixt_ref[ATTN_W + c0:ATTN_W + c0 + RET_HEAD_DIM, :] = res.astype(bf16)

    def ffn_epilogue():
        o_ref[...] = x1_ref[...] + _rms(o_ref[...], g4_ref[...])

    def next_input_norm():
        h_ref[...] = _rms(xn_ref[...], g1_ref[...]).astype(bf16)

    attn_ids = [(b, hk) for b in range(T // W) for hk in range(N_KV_HEADS)]
    pairs = ([(attn_front, attn_back, ids) for ids in (attn_ids[0:2], attn_ids[2:4])]
             + [(ret_front, ret_back, ids) for ids in ([(0,), (1,)], [(2,), (3,)])])
    for n, (front, back, ids) in enumerate(pairs):
        fronts = [front(*i) for i in ids]
        ffn_down_unit(n)
        for i, f in zip(ids, fronts):
            back(*i, *(f if isinstance(f, tuple) else (f,)))
        if n == len(pairs) - 2:
            next_input_norm()
    ffn_epilogue()

    mixed = _dot_tn(mixt_ref[...], w_out_ref[...])
    x1_ref[...] = x_ref[...] + _rms(mixed, g2_ref[...])


def kernel(x, mix_pre_norm, w_in, attn_sinks, w_out, mix_post_norm, ffn_pre_norm, w_up,
           conv_w, conv_b, w_down, ffn_post_norm):
    batch, seq, d_model = x.shape
    depth = w_in.shape[0]
    assert d_model == D_MODEL and seq % SEQ_BLOCK == 0
    tabs = _retention_tables(seq)
    n_steps = seq // SEQ_BLOCK
    bf16 = jnp.bfloat16

    def resident(shape):
        nd = len(shape)
        return pl.BlockSpec(shape, lambda i, _n=nd: (0,) * _n, pipeline_mode=pl.Buffered(1))

    in_specs = [
        pl.BlockSpec((SEQ_BLOCK, D_MODEL), lambda i: (jnp.minimum(i, n_steps - 1), 0)),
        pl.BlockSpec((SEQ_BLOCK, D_MODEL), lambda i: (jnp.minimum(i + 1, n_steps - 1), 0)),
        resident((1, D_MODEL)), resident((1, D_MODEL)), resident((1, D_MODEL)), resident((1, D_MODEL)),
        resident((N_KV_HEADS, Q_PER_KV * WINDOW)),
        pl.BlockSpec(memory_space=pl.ANY), pl.BlockSpec(memory_space=pl.ANY),
        pl.BlockSpec(memory_space=pl.ANY), pl.BlockSpec(memory_space=pl.ANY),
        resident((3, 2 * D_FF)), resident((1, 2 * D_FF)),
        resident((n_steps, LANES)), resident((n_steps, LANES)),
        resident((SEQ_BLOCK, LANES)), resident((SEQ_BLOCK, LANES)),
        resident((N_RET_HEADS, SEQ_BLOCK, SEQ_BLOCK)),
        resident((N_RET_HEADS, SEQ_BLOCK, LANES)), resident((N_RET_HEADS, SEQ_BLOCK, LANES)),
    ]
    n_conv_slabs = 2 * D_FF // LANES
    call = pl.pallas_call(
        functools.partial(_layer_kernel, tabs["chunk_decay"], n_steps),
        grid=(n_steps + 1,),
        in_specs=in_specs,
        out_specs=pl.BlockSpec((SEQ_BLOCK, D_MODEL), lambda i: (jnp.maximum(i - 1, 0), 0)),
        out_shape=jax.ShapeDtypeStruct((seq, D_MODEL), jnp.float32),
        scratch_shapes=[
            pltpu.VMEM((T_ROWS, D_MODEL), bf16),
            pltpu.VMEM((D_MODEL, S_COLS), bf16),
            pltpu.VMEM((D_MODEL, D_MODEL), bf16),
            pltpu.VMEM((D_MODEL, 2 * D_FF), bf16),
            pltpu.VMEM((D_FF, D_MODEL), bf16),
            pltpu.SemaphoreType.DMA((2 * n_conv_slabs,)),
            pltpu.VMEM((WINDOW, KV_W), jnp.float32),
            pltpu.VMEM((KV_W, WINDOW), jnp.float32),
            pltpu.VMEM((N_RET_HEADS, RET_HEAD_DIM, RET_HEAD_DIM), jnp.float32),
            pltpu.VMEM((n_conv_slabs, CONV_ROW0 + 2 * SEQ_BLOCK, LANES), jnp.float32),
            pltpu.VMEM((D_MODEL, SEQ_BLOCK), bf16),
            pltpu.VMEM((SEQ_BLOCK, D_FF), bf16),
            pltpu.VMEM((SEQ_BLOCK, D_MODEL), jnp.float32),
            pltpu.VMEM((SEQ_BLOCK, D_MODEL), bf16),
        ],
        compiler_params=pltpu.CompilerParams(
            dimension_semantics=("arbitrary",), vmem_limit_bytes=VMEM_LIMIT_BYTES),
        name="hybrid_layer",
    )

    outs = []
    for bi in range(batch):
        xb = x[bi]
        for l in range(depth):
            sink_rows = jnp.repeat(attn_sinks[l].reshape(N_KV_HEADS, Q_PER_KV), WINDOW, axis=1)
            xb = call(
                xb, xb,
                mix_pre_norm[l].reshape(1, D_MODEL), mix_post_norm[l].reshape(1, D_MODEL),
                ffn_pre_norm[l].reshape(1, D_MODEL), ffn_post_norm[l].reshape(1, D_MODEL),
                sink_rows,
                w_in[l], w_out[l], w_up[l], w_down[l],
                conv_w[l], conv_b[l].reshape(1, 2 * D_FF),
                tabs["cos_base"], tabs["sin_base"], tabs["cos_intra"], tabs["sin_intra"],
                tabs["decay_t"], tabs["xi"], tabs["zeta"],
            )
        outs.append(xb)
    return outs[0][None] if batch == 1 else jnp.stack(outs, axis=0)
```

```python
import functools

import numpy as np
import jax
import jax.numpy as jnp
from jax import lax
from jax.experimental import pallas as pl
from jax.experimental.pallas import tpu as pltpu

D_MODEL = 1024
HEAD_DIM = 64
ATTN_W = 512
N_ATTN_HEADS = 8
N_KV_HEADS = 2
Q_PER_KV = N_ATTN_HEADS // N_KV_HEADS
KV_W = N_KV_HEADS * HEAD_DIM
WINDOW = 128
RET_W = 512
N_RET_HEADS = 4
RET_HEAD_DIM = 128
D_FF = 2816
RMS_EPS = 1e-6
GN_EPS = 1e-6
MASK_VALUE = -1e30
GELU_C = 0.7978845608028654

LANES = 128
SUBLANES = 8
MXU_TILE = 256
SEQ_BLOCK = 256
FF_TILE = MXU_TILE
N_FF_TILES = D_FF // FF_TILE
CONV_ROW0 = 2 * SUBLANES
STAGE_ROWS = SEQ_BLOCK
V7X_VMEM_BYTES = 64 * 1024 * 1024
VMEM_LIMIT_BYTES = V7X_VMEM_BYTES - 4 * 1024 * 1024

T_QA, T_KA, T_VA, T_VR, T_GR = 0, ATTN_W, ATTN_W + KV_W, ATTN_W + 2 * KV_W, ATTN_W + 2 * KV_W + RET_W
T_ROWS = ATTN_W + 2 * KV_W + 2 * RET_W
S_QR, S_KR = 0, RET_W
S_COLS = 2 * RET_W


def _retention_tables(seq):
    c = SEQ_BLOCK
    n_steps = seq // c
    angle = 1.0 / np.power(10000.0, np.linspace(0.0, 1.0, RET_HEAD_DIM // 2))
    angle = np.repeat(angle, 2)
    base = (np.arange(n_steps, dtype=np.float64) * c)[:, None] * angle[None]
    intra = np.arange(c, dtype=np.float64)[:, None] * angle[None]
    gamma = 1.0 - np.power(2.0, -5.0 - np.arange(N_RET_HEADS, dtype=np.float64))
    k_scale = RET_HEAD_DIM ** -0.5
    idx = np.arange(c, dtype=np.float64)
    rel = idx[None, :] - idx[:, None]
    decay_t = np.where(rel[None] >= 0, gamma[:, None, None] ** np.maximum(rel, 0.0)[None], 0.0)
    xi = gamma[:, None] ** (idx[None, :] + 1.0)
    zeta = gamma[:, None] ** (c - 1.0 - idx[None, :])
    f32 = lambda a: jnp.asarray(a, dtype=jnp.float32)
    return dict(
        cos_base=f32(np.cos(base)), sin_base=f32(np.sin(base)),
        cos_intra=f32(np.cos(intra)), sin_intra=f32(np.sin(intra)),
        decay_t=f32(k_scale * decay_t),
        xi=f32(np.broadcast_to(xi[:, :, None], (N_RET_HEADS, c, LANES))),
        zeta=f32(np.broadcast_to(k_scale * zeta[:, :, None], (N_RET_HEADS, c, LANES))),
        chunk_decay=[float(g ** c) for g in gamma],
    )


def _rms(x, w):
    return x * lax.rsqrt(jnp.mean(x * x, axis=-1, keepdims=True) + RMS_EPS) * w


def _dot(a, b):
    return jnp.dot(a, b, preferred_element_type=jnp.float32)


def _dot_nt(a, b):
    return lax.dot_general(a, b, (((1,), (1,)), ((), ())), preferred_element_type=jnp.float32)


def _dot_tn(a, b):
    return lax.dot_general(a, b, (((0,), (0,)), ((), ())), preferred_element_type=jnp.float32)


def _up_col(c):
    part, c = divmod(c, D_FF)
    return (c // FF_TILE) * 2 * FF_TILE + part * FF_TILE + c % FF_TILE


def _stage_weights(w_in_hbm, w_out_hbm, w_up_hbm, w_down_hbm,
                   w_t_ref, w_s_ref, w_out_ref, w_up_ref, w_down_ref, stage_ref, sem):
    bf16 = jnp.bfloat16
    R = STAGE_ROWS

    def plain(dst_ref, dst_col, scale=None):
        def store(tile, row0):
            dst_ref[row0:row0 + R, dst_col:dst_col + LANES] = (tile if scale is None else tile * scale).astype(bf16)
        return store

    def transposed(dst_row, scale=None):
        def store(tile, row0):
            tile = tile if scale is None else tile * scale
            w_t_ref[dst_row:dst_row + LANES, row0:row0 + R] = jnp.transpose(tile).astype(bf16)
        return store

    in_slabs = ([transposed(T_QA + i * LANES, HEAD_DIM ** -0.5) for i in range(ATTN_W // LANES)]
                + [transposed(T_KA), transposed(T_VA)]
                + [plain(w_s_ref, S_QR + i * LANES) for i in range(RET_W // LANES)]
                + [plain(w_s_ref, S_KR + i * LANES) for i in range(RET_W // LANES)]
                + [transposed(T_VR + i * LANES) for i in range(RET_W // LANES)]
                + [transposed(T_GR + i * LANES) for i in range(RET_W // LANES)])
    tiles = []
    for src, stores in ((w_in_hbm, in_slabs),
                        (w_out_hbm, [plain(w_out_ref, c) for c in range(0, D_MODEL, LANES)]),
                        (w_up_hbm, [plain(w_up_ref, _up_col(c)) for c in range(0, 2 * D_FF, LANES)]),
                        (w_down_hbm, [plain(w_down_ref, c, 0.5) for c in range(0, D_MODEL, LANES)])):
        for row0 in range(0, src.shape[0], R):
            tiles += [(src, row0, i * LANES, store) for i, store in enumerate(stores)]

    n_slabs = stage_ref.shape[0]
    n_slots = 2 * n_slabs

    def copy(t):
        src, row0, col0, _ = tiles[t]
        slot = t % n_slots
        return pltpu.make_async_copy(src.at[pl.ds(row0, R), pl.ds(col0, LANES)],
                                     stage_ref.at[slot % n_slabs, pl.ds((slot // n_slabs) * R, R), :],
                                     sem.at[slot])

    for t in range(min(n_slots, len(tiles))):
        copy(t).start(priority=t % 2)
    for t, (_, row0, _, store) in enumerate(tiles):
        copy(t).wait()
        slot = t % n_slots
        half = slot // n_slabs
        store(stage_ref[slot % n_slabs, half * R:(half + 1) * R, :], row0)
        if t + n_slots < len(tiles):
            copy(t + n_slots).start(priority=t % 2)


def _layer_kernel(chunk_decay, n_blocks,
                  x_ref, xn_ref, g1_ref, g2_ref, g3_ref, g4_ref, sink_ref,
                  w_in_hbm, w_out_hbm, w_up_hbm, w_down_hbm, cw_ref, cb_ref,
                  cosb_ref, sinb_ref, cosi_ref, sini_ref, decay_ref, xi_ref, zeta_ref,
                  o_ref,
                  w_t_ref, w_s_ref, w_out_ref, w_up_ref, w_down_ref, stage_sem,
                  kprev_ref, vtprev_ref, state_ref, pbuf_ref, mixt_ref, y_ref, x1_ref, h_ref):
    step = pl.program_id(0)
    bf16 = jnp.bfloat16
    T = SEQ_BLOCK
    W = WINDOW

    @pl.when(step == 0)
    def _init():
        _stage_weights(w_in_hbm, w_out_hbm, w_up_hbm, w_down_hbm,
                       w_t_ref, w_s_ref, w_out_ref, w_up_ref, w_down_ref, pbuf_ref, stage_sem)
        kprev_ref[...] = jnp.zeros_like(kprev_ref)
        vtprev_ref[...] = jnp.zeros_like(vtprev_ref)
        state_ref[...] = jnp.zeros_like(state_ref)
        pbuf_ref[...] = jnp.zeros_like(pbuf_ref)
        x1_ref[...] = jnp.zeros_like(x1_ref)
        h_ref[...] = _rms(x_ref[...], g1_ref[...]).astype(bf16)

    blk = jnp.minimum(step, n_blocks - 1)

    h = h_ref[...]
    proj_t = _dot_nt(w_t_ref[...], h)
    proj_s = _dot(h, w_s_ref[...])
    h2 = _rms(x1_ref[...], g3_ref[...]).astype(bf16)

    def conv(p, slab, c0):
        pbuf_ref[slab, pl.ds(CONV_ROW0, T, stride=2), :] = p
        p1 = pbuf_ref[slab, pl.ds(CONV_ROW0 - 2, T, stride=2), :]
        p2 = pbuf_ref[slab, pl.ds(CONV_ROW0 - 4, T, stride=2), :]
        pbuf_ref[slab, CONV_ROW0 - 4:CONV_ROW0, :] = pbuf_ref[slab, CONV_ROW0 + 2 * T - 4:CONV_ROW0 + 2 * T, :]
        return (cw_ref[2:3, c0:c0 + LANES] * p + cw_ref[1:2, c0:c0 + LANES] * p1
                + cw_ref[0:1, c0:c0 + LANES] * p2 + cb_ref[:, c0:c0 + LANES])

    def ffn_up_unit(j):
        c_gate, c_val = j * FF_TILE, D_FF + j * FF_TILE
        p = _dot(h2, w_up_ref[:, 2 * c_gate:2 * c_gate + 2 * FF_TILE])
        p_gate, p_val = p[:, :FF_TILE], p[:, FF_TILE:]
        for half in range(FF_TILE // LANES):
            l0 = half * LANES
            ug = conv(p_gate[:, l0:l0 + LANES], (c_gate + l0) // LANES, c_gate + l0)
            uv = conv(p_val[:, l0:l0 + LANES], (c_val + l0) // LANES, c_val + l0)
            t = jnp.tanh(ug * (ug * ug * (GELU_C * 0.044715) + GELU_C))
            y_ref[:, c_gate + l0:c_gate + l0 + LANES] = (ug * (1.0 + t) * uv).astype(bf16)

    def ffn_down_unit(n):
        c0 = n * MXU_TILE
        o_ref[:, c0:c0 + MXU_TILE] = _dot(y_ref[...], w_down_ref[:, c0:c0 + MXU_TILE])

    for j in range(N_FF_TILES):
        ffn_up_unit(j)

    key_idx = lax.broadcasted_iota(jnp.int32, (W, Q_PER_KV * W), 0)
    qry_idx = lax.broadcasted_iota(jnp.int32, (W, Q_PER_KV * W), 1) & (W - 1)
    from_prev = key_idx > qry_idx
    first_block_penalty = jnp.where(step == 0, MASK_VALUE, 0.0)
    k_a = jnp.transpose(proj_t[T_KA:T_KA + KV_W, :])
    vt_a = proj_t[T_VA:T_VA + KV_W, :]
    zeros_half = jnp.zeros((HEAD_DIM, Q_PER_KV * W), bf16)
    kk_first = jnp.concatenate([kprev_ref[...], k_a[:W]], axis=0)
    vvt_first = jnp.concatenate([vtprev_ref[...], vt_a[:, :W]], axis=1)
    kprev_ref[...] = k_a[T - W:]
    vtprev_ref[...] = vt_a[:, T - W:]

    def attn_front(b, hk):
        t0 = b * W
        kk = (kk_first if b == 0 else k_a[t0 - W:t0 + W]).astype(bf16)
        h0 = hk * Q_PER_KV
        qt = jnp.concatenate(
            [proj_t[T_QA + (h0 + g) * HEAD_DIM:T_QA + (h0 + g + 1) * HEAD_DIM, t0:t0 + W]
             for g in range(Q_PER_KV)], axis=1).astype(bf16)
        rhs = jnp.concatenate([qt, zeros_half] if hk == 0 else [zeros_half, qt], axis=0)
        return _dot(kk, rhs)

    def attn_back(b, hk, st):
        t0 = b * W
        vvt = (vvt_first if b == 0 else vt_a[:, t0 - W:t0 + W]).astype(bf16)
        h0 = hk * Q_PER_KV
        s_prev = st[:W] + first_block_penalty if b == 0 else st[:W]
        s = jnp.where(from_prev, s_prev, st[W:])
        sink = sink_ref[hk:hk + 1, :]
        m = jnp.maximum(jnp.max(s, axis=0, keepdims=True), sink)
        p = jnp.exp(s - m)
        denom = jnp.sum(p, axis=0, keepdims=True) + jnp.exp(sink - m)
        pt = jnp.concatenate([jnp.where(from_prev, p, 0.0).astype(bf16),
                              jnp.where(from_prev, 0.0, p).astype(bf16)], axis=0)
        out_t = _dot(vvt[hk * HEAD_DIM:(hk + 1) * HEAD_DIM, :], pt) * (1.0 / denom)
        for g in range(Q_PER_KV):
            r0 = (h0 + g) * HEAD_DIM
            mixt_ref[r0:r0 + HEAD_DIM, t0:t0 + W] = out_t[:, g * W:(g + 1) * W].astype(bf16)

    lane = lax.broadcasted_iota(jnp.int32, (1, LANES), 1)
    lane_even = (lane & 1) == 0
    cos_b = cosb_ref[pl.ds(blk, 1), :]
    sin_b = sinb_ref[pl.ds(blk, 1), :]
    cos_i, sin_i = cosi_ref[...], sini_ref[...]
    cos = cos_b * cos_i - sin_b * sin_i
    sin = sin_b * cos_i + cos_b * sin_i
    sin_signed = jnp.where(lane_even, -sin, sin)

    def rotate(t):
        swapped = jnp.where(lane_even, pltpu.roll(t, LANES - 1, 1), pltpu.roll(t, 1, 1))
        return t * cos + swapped * sin_signed

    def ret_front(hh):
        c0 = hh * RET_HEAD_DIM
        q = rotate(proj_s[:, S_QR + c0:S_QR + c0 + RET_HEAD_DIM])
        k = rotate(proj_s[:, S_KR + c0:S_KR + c0 + RET_HEAD_DIM])
        return q, k, _dot_nt(k.astype(bf16), q.astype(bf16))

    def ret_back(hh, q, k, kq_t):
        c0 = hh * RET_HEAD_DIM
        vt = proj_t[T_VR + c0:T_VR + c0 + RET_HEAD_DIM, :].astype(bf16)
        gate_t = proj_t[T_GR + c0:T_GR + c0 + RET_HEAD_DIM, :]
        inner_t = kq_t * decay_ref[hh]
        state_t = state_ref[hh]
        o_t = (_dot(vt, inner_t.astype(bf16))
               + _dot_nt(state_t.astype(bf16), (q * xi_ref[hh]).astype(bf16)))
        state_ref[hh] = chunk_decay[hh] * state_t + _dot(vt, (k * zeta_ref[hh]).astype(bf16))
        mu = jnp.mean(o_t, axis=0, keepdims=True)
        d = o_t - mu
        var = jnp.mean(d * d, axis=0, keepdims=True)
        res = d * lax.rsqrt(var + GN_EPS) * (gate_t * (1.0 / (1.0 + jnp.exp(-gate_t))))
        mixt_ref[ATTN_W + c0:ATTN_W + c0 + RET_HEAD_DIM, :] = res.astype(bf16)

    def ffn_epilogue():
        o_ref[...] = x1_ref[...] + _rms(o_ref[...], g4_ref[...])

    def next_input_norm():
        h_ref[...] = _rms(xn_ref[...], g1_ref[...]).astype(bf16)

    attn_ids = [(b, hk) for b in range(T // W) for hk in range(N_KV_HEADS)]
    pairs = ([(attn_front, attn_back, ids) for ids in (attn_ids[0:2], attn_ids[2:4])]
             + [(ret_front, ret_back, ids) for ids in ([(0,), (1,)], [(2,), (3,)])])
    for n, (front, back, ids) in enumerate(pairs):
        fronts = [front(*i) for i in ids]
        ffn_down_unit(n)
        for i, f in zip(ids, fronts):
            back(*i, *(f if isinstance(f, tuple) else (f,)))
        if n == len(pairs) - 2:
            next_input_norm()
    ffn_epilogue()

    mixed = _dot_tn(mixt_ref[...], w_out_ref[...])
    x1_ref[...] = x_ref[...] + _rms(mixed, g2_ref[...])


def kernel(x, mix_pre_norm, w_in, attn_sinks, w_out, mix_post_norm, ffn_pre_norm, w_up,
           conv_w, conv_b, w_down, ffn_post_norm):
    batch, seq, d_model = x.shape
    depth = w_in.shape[0]
    assert d_model == D_MODEL and seq % SEQ_BLOCK == 0
    tabs = _retention_tables(seq)
    n_steps = seq // SEQ_BLOCK
    bf16 = jnp.bfloat16

    def resident(shape):
        nd = len(shape)
        return pl.BlockSpec(shape, lambda i, _n=nd: (0,) * _n, pipeline_mode=pl.Buffered(1))

    in_specs = [
        pl.BlockSpec((SEQ_BLOCK, D_MODEL), lambda i: (jnp.minimum(i, n_steps - 1), 0)),
        pl.BlockSpec((SEQ_BLOCK, D_MODEL), lambda i: (jnp.minimum(i + 1, n_steps - 1), 0)),
        resident((1, D_MODEL)), resident((1, D_MODEL)), resident((1, D_MODEL)), resident((1, D_MODEL)),
        resident((N_KV_HEADS, Q_PER_KV * WINDOW)),
        pl.BlockSpec(memory_space=pl.ANY), pl.BlockSpec(memory_space=pl.ANY),
        pl.BlockSpec(memory_space=pl.ANY), pl.BlockSpec(memory_space=pl.ANY),
        resident((3, 2 * D_FF)), resident((1, 2 * D_FF)),
        resident((n_steps, LANES)), resident((n_steps, LANES)),
        resident((SEQ_BLOCK, LANES)), resident((SEQ_BLOCK, LANES)),
        resident((N_RET_HEADS, SEQ_BLOCK, SEQ_BLOCK)),
        resident((N_RET_HEADS, SEQ_BLOCK, LANES)), resident((N_RET_HEADS, SEQ_BLOCK, LANES)),
    ]
    n_conv_slabs = 2 * D_FF // LANES
    call = pl.pallas_call(
        functools.partial(_layer_kernel, tabs["chunk_decay"], n_steps),
        grid=(n_steps + 1,),
        in_specs=in_specs,
        out_specs=pl.BlockSpec((SEQ_BLOCK, D_MODEL), lambda i: (jnp.maximum(i - 1, 0), 0)),
        out_shape=jax.ShapeDtypeStruct((seq, D_MODEL), jnp.float32),
        scratch_shapes=[
            pltpu.VMEM((T_ROWS, D_MODEL), bf16),
            pltpu.VMEM((D_MODEL, S_COLS), bf16),
            pltpu.VMEM((D_MODEL, D_MODEL), bf16),
            pltpu.VMEM((D_MODEL, 2 * D_FF), bf16),
            pltpu.VMEM((D_FF, D_MODEL), bf16),
            pltpu.SemaphoreType.DMA((2 * n_conv_slabs,)),
            pltpu.VMEM((WINDOW, KV_W), jnp.float32),
            pltpu.VMEM((KV_W, WINDOW), jnp.float32),
            pltpu.VMEM((N_RET_HEADS, RET_HEAD_DIM, RET_HEAD_DIM), jnp.float32),
            pltpu.VMEM((n_conv_slabs, CONV_ROW0 + 2 * SEQ_BLOCK, LANES), jnp.float32),
            pltpu.VMEM((D_MODEL, SEQ_BLOCK), bf16),
            pltpu.VMEM((SEQ_BLOCK, D_FF), bf16),
            pltpu.VMEM((SEQ_BLOCK, D_MODEL), jnp.float32),
            pltpu.VMEM((SEQ_BLOCK, D_MODEL), bf16),
        ],
        compiler_params=pltpu.CompilerParams(
            dimension_semantics=("arbitrary",), vmem_limit_bytes=VMEM_LIMIT_BYTES),
        name="hybrid_layer",
    )

    outs = []
    for bi in range(batch):
        xb = x[bi]
        for l in range(depth):
            sink_rows = jnp.repeat(attn_sinks[l].reshape(N_KV_HEADS, Q_PER_KV), WINDOW, axis=1)
            xb = call(
                xb, xb,
                mix_pre_norm[l].reshape(1, D_MODEL), mix_post_norm[l].reshape(1, D_MODEL),
                ffn_pre_norm[l].reshape(1, D_MODEL), ffn_post_norm[l].reshape(1, D_MODEL),
                sink_rows,
                w_in[l], w_out[l], w_up[l], w_down[l],
                conv_w[l], conv_b[l].reshape(1, 2 * D_FF),
                tabs["cos_base"], tabs["sin_base"], tabs["cos_intra"], tabs["sin_intra"],
                tabs["decay_t"], tabs["xi"], tabs["zeta"],
            )
        outs.append(xb)
    return outs[0][None] if batch == 1 else jnp.stack(outs, axis=0)
```

```python
import functools

import numpy as np
import jax
import jax.numpy as jnp
from jax import lax
from jax.experimental import pallas as pl
from jax.experimental.pallas import tpu as pltpu

D_MODEL = 1024
HEAD_DIM = 64
ATTN_W = 512
N_ATTN_HEADS = 8
N_KV_HEADS = 2
Q_PER_KV = N_ATTN_HEADS // N_KV_HEADS
KV_W = N_KV_HEADS * HEAD_DIM
WINDOW = 128
RET_W = 512
N_RET_HEADS = 4
RET_HEAD_DIM = 128
D_FF = 2816
RMS_EPS = 1e-6
GN_EPS = 1e-6
MASK_VALUE = -1e30
GELU_C = 0.7978845608028654

LANES = 128
SUBLANES = 8
MXU_TILE = 256
SEQ_BLOCK = 256
FF_TILE = MXU_TILE
N_FF_TILES = D_FF // FF_TILE
CONV_ROW0 = 2 * SUBLANES
STAGE_ROWS = SEQ_BLOCK
V7X_VMEM_BYTES = 64 * 1024 * 1024
VMEM_LIMIT_BYTES = V7X_VMEM_BYTES - 4 * 1024 * 1024

T_QA, T_KA, T_VA, T_VR, T_GR = 0, ATTN_W, ATTN_W + KV_W, ATTN_W + 2 * KV_W, ATTN_W + 2 * KV_W + RET_W
T_ROWS = ATTN_W + 2 * KV_W + 2 * RET_W
S_QR, S_KR = 0, RET_W
S_COLS = 2 * RET_W


def _retention_tables(seq):
    c = SEQ_BLOCK
    n_steps = seq // c
    angle = 1.0 / np.power(10000.0, np.linspace(0.0, 1.0, RET_HEAD_DIM // 2))
    angle = np.repeat(angle, 2)
    base = (np.arange(n_steps, dtype=np.float64) * c)[:, None] * angle[None]
    intra = np.arange(c, dtype=np.float64)[:, None] * angle[None]
    gamma = 1.0 - np.power(2.0, -5.0 - np.arange(N_RET_HEADS, dtype=np.float64))
    k_scale = RET_HEAD_DIM ** -0.5
    idx = np.arange(c, dtype=np.float64)
    rel = idx[None, :] - idx[:, None]
    decay_t = np.where(rel[None] >= 0, gamma[:, None, None] ** np.maximum(rel, 0.0)[None], 0.0)
    xi = gamma[:, None] ** (idx[None, :] + 1.0)
    zeta = gamma[:, None] ** (c - 1.0 - idx[None, :])
    f32 = lambda a: jnp.asarray(a, dtype=jnp.float32)
    return dict(
        cos_base=f32(np.cos(base)), sin_base=f32(np.sin(base)),
        cos_intra=f32(np.cos(intra)), sin_intra=f32(np.sin(intra)),
        decay_t=f32(k_scale * decay_t),
        xi=f32(np.broadcast_to(xi[:, :, None], (N_RET_HEADS, c, LANES))),
        zeta=f32(np.broadcast_to(k_scale * zeta[:, :, None], (N_RET_HEADS, c, LANES))),
        chunk_decay=[float(g ** c) for g in gamma],
    )


def _rms(x, w):
    return x * lax.rsqrt(jnp.mean(x * x, axis=-1, keepdims=True) + RMS_EPS) * w


def _dot(a, b):
    return jnp.dot(a, b, preferred_element_type=jnp.float32)


def _dot_nt(a, b):
    return lax.dot_general(a, b, (((1,), (1,)), ((), ())), preferred_element_type=jnp.float32)


def _dot_tn(a, b):
    return lax.dot_general(a, b, (((0,), (0,)), ((), ())), preferred_element_type=jnp.float32)


def _up_col(c):
    part, c = divmod(c, D_FF)
    return (c // FF_TILE) * 2 * FF_TILE + part * FF_TILE + c % FF_TILE


def _stage_weights(w_in_hbm, w_out_hbm, w_up_hbm, w_down_hbm,
                   w_t_ref, w_s_ref, w_out_ref, w_up_ref, w_down_ref, stage_ref, sem):
    bf16 = jnp.bfloat16
    R = STAGE_ROWS

    def plain(dst_ref, dst_col, scale=None):
        def store(tile, row0):
            dst_ref[row0:row0 + R, dst_col:dst_col + LANES] = (tile if scale is None else tile * scale).astype(bf16)
        return store

    def transposed(dst_row, scale=None):
        def store(tile, row0):
            tile = tile if scale is None else tile * scale
            w_t_ref[dst_row:dst_row + LANES, row0:row0 + R] = jnp.transpose(tile).astype(bf16)
        return store

    in_slabs = ([transposed(T_QA + i * LANES, HEAD_DIM ** -0.5) for i in range(ATTN_W // LANES)]
                + [transposed(T_KA), transposed(T_VA)]
                + [plain(w_s_ref, S_QR + i * LANES) for i in range(RET_W // LANES)]
                + [plain(w_s_ref, S_KR + i * LANES) for i in range(RET_W // LANES)]
                + [transposed(T_VR + i * LANES) for i in range(RET_W // LANES)]
                + [transposed(T_GR + i * LANES) for i in range(RET_W // LANES)])
    tiles = []
    for src, stores in ((w_in_hbm, in_slabs),
                        (w_out_hbm, [plain(w_out_ref, c) for c in range(0, D_MODEL, LANES)]),
                        (w_up_hbm, [plain(w_up_ref, _up_col(c)) for c in range(0, 2 * D_FF, LANES)]),
                        (w_down_hbm, [plain(w_down_ref, c, 0.5) for c in range(0, D_MODEL, LANES)])):
        for row0 in range(0, src.shape[0], R):
            tiles += [(src, row0, i * LANES, store) for i, store in enumerate(stores)]

    n_slabs = stage_ref.shape[0]
    n_slots = 2 * n_slabs

    def copy(t):
        src, row0, col0, _ = tiles[t]
        slot = t % n_slots
        return pltpu.make_async_copy(src.at[pl.ds(row0, R), pl.ds(col0, LANES)],
                                     stage_ref.at[slot % n_slabs, pl.ds((slot // n_slabs) * R, R), :],
                                     sem.at[slot])

    for t in range(min(n_slots, len(tiles))):
        copy(t).start()
    for t, (_, row0, _, store) in enumerate(tiles):
        copy(t).wait()
        slot = t % n_slots
        half = slot // n_slabs
        store(stage_ref[slot % n_slabs, half * R:(half + 1) * R, :], row0)
        if t + n_slots < len(tiles):
            copy(t + n_slots).start()


def _layer_kernel(chunk_decay, n_blocks,
                  xp_ref, xn_ref, g1_ref, g2_ref, g3_ref, g4_ref, sink_ref,
                  w_in_hbm, w_out_hbm, w_up_hbm, w_down_hbm, cw_ref, cb_ref,
                  cosb_ref, sinb_ref, cosi_ref, sini_ref, decay_ref, xi_ref, zeta_ref,
                  o_ref,
                  w_t_ref, w_s_ref, w_out_ref, w_up_ref, w_down_ref, stage_sem,
                  kprev_ref, vtprev_ref, state_ref, pbuf_ref, mixt_ref, y_ref, x1_ref, h_ref):
    step = pl.program_id(0)
    bf16 = jnp.bfloat16
    T = SEQ_BLOCK
    W = WINDOW

    @pl.when(step == 0)
    def _init():
        _stage_weights(w_in_hbm, w_out_hbm, w_up_hbm, w_down_hbm,
                       w_t_ref, w_s_ref, w_out_ref, w_up_ref, w_down_ref, pbuf_ref, stage_sem)
        kprev_ref[...] = jnp.zeros_like(kprev_ref)
        vtprev_ref[...] = jnp.zeros_like(vtprev_ref)
        state_ref[...] = jnp.zeros_like(state_ref)
        pbuf_ref[...] = jnp.zeros_like(pbuf_ref)
        x1_ref[...] = jnp.zeros_like(x1_ref)
        h_ref[...] = _rms(xp_ref[...], g1_ref[...]).astype(bf16)

    blk = jnp.minimum(step, n_blocks - 1)
    prev_valid = jnp.where(step > 0, 1.0, 0.0)

    h = h_ref[...]
    proj_t = _dot_nt(w_t_ref[...], h)
    proj_s = _dot(h, w_s_ref[...])
    x1 = (xp_ref[...] + _rms(x1_ref[...], g2_ref[...])) * prev_valid
    x1_ref[...] = x1
    h2 = _rms(x1, g3_ref[...]).astype(bf16)

    def conv(p, slab, c0):
        pbuf_ref[slab, pl.ds(CONV_ROW0, T, stride=2), :] = p
        p1 = pbuf_ref[slab, pl.ds(CONV_ROW0 - 2, T, stride=2), :]
        p2 = pbuf_ref[slab, pl.ds(CONV_ROW0 - 4, T, stride=2), :]
        pbuf_ref[slab, CONV_ROW0 - 4:CONV_ROW0, :] = pbuf_ref[slab, CONV_ROW0 + 2 * T - 4:CONV_ROW0 + 2 * T, :]
        return (cw_ref[2:3, c0:c0 + LANES] * p + cw_ref[1:2, c0:c0 + LANES] * p1
                + cw_ref[0:1, c0:c0 + LANES] * p2 + cb_ref[:, c0:c0 + LANES])

    def ffn_up_unit(j):
        c_gate, c_val = j * FF_TILE, D_FF + j * FF_TILE
        p = _dot(h2, w_up_ref[:, 2 * c_gate:2 * c_gate + 2 * FF_TILE])
        p_gate, p_val = p[:, :FF_TILE], p[:, FF_TILE:]
        for half in range(FF_TILE // LANES):
            l0 = half * LANES
            ug = conv(p_gate[:, l0:l0 + LANES], (c_gate + l0) // LANES, c_gate + l0)
            uv = conv(p_val[:, l0:l0 + LANES], (c_val + l0) // LANES, c_val + l0)
            t = jnp.tanh(ug * (ug * ug * (GELU_C * 0.044715) + GELU_C))
            y_ref[:, c_gate + l0:c_gate + l0 + LANES] = (ug * (1.0 + t) * uv).astype(bf16)

    def ffn_down_unit(n):
        c0 = n * MXU_TILE
        o_ref[:, c0:c0 + MXU_TILE] = _dot(y_ref[...], w_down_ref[:, c0:c0 + MXU_TILE])

    for j in range(N_FF_TILES):
        ffn_up_unit(j)

    key_idx = lax.broadcasted_iota(jnp.int32, (W, Q_PER_KV * W), 0)
    qry_idx = lax.broadcasted_iota(jnp.int32, (W, Q_PER_KV * W), 1) & (W - 1)
    from_prev = key_idx > qry_idx
    first_block_penalty = jnp.where(step == 0, MASK_VALUE, 0.0)
    k_a = jnp.transpose(proj_t[T_KA:T_KA + KV_W, :])
    vt_a = proj_t[T_VA:T_VA + KV_W, :]
    zeros_half = jnp.zeros((HEAD_DIM, Q_PER_KV * W), bf16)
    kk_first = jnp.concatenate([kprev_ref[...], k_a[:W]], axis=0)
    vvt_first = jnp.concatenate([vtprev_ref[...], vt_a[:, :W]], axis=1)
    kprev_ref[...] = k_a[T - W:]
    vtprev_ref[...] = vt_a[:, T - W:]

    def attn_front(b, hk):
        t0 = b * W
        kk = (kk_first if b == 0 else k_a[t0 - W:t0 + W]).astype(bf16)
        h0 = hk * Q_PER_KV
        qt = jnp.concatenate(
            [proj_t[T_QA + (h0 + g) * HEAD_DIM:T_QA + (h0 + g + 1) * HEAD_DIM, t0:t0 + W]
             for g in range(Q_PER_KV)], axis=1).astype(bf16)
        rhs = jnp.concatenate([qt, zeros_half] if hk == 0 else [zeros_half, qt], axis=0)
        return _dot(kk, rhs)

    def attn_back(b, hk, st):
        t0 = b * W
        vvt = (vvt_first if b == 0 else vt_a[:, t0 - W:t0 + W]).astype(bf16)
        h0 = hk * Q_PER_KV
        s_prev = st[:W] + first_block_penalty if b == 0 else st[:W]
        s = jnp.where(from_prev, s_prev, st[W:])
        sink = sink_ref[hk:hk + 1, :]
        m = jnp.maximum(jnp.max(s, axis=0, keepdims=True), sink)
        p = jnp.exp(s - m)
        denom = jnp.sum(p, axis=0, keepdims=True) + jnp.exp(sink - m)
        pt = jnp.concatenate([jnp.where(from_prev, p, 0.0).astype(bf16),
                              jnp.where(from_prev, 0.0, p).astype(bf16)], axis=0)
        out_t = _dot(vvt[hk * HEAD_DIM:(hk + 1) * HEAD_DIM, :], pt) * (1.0 / denom)
        for g in range(Q_PER_KV):
            r0 = (h0 + g) * HEAD_DIM
            mixt_ref[r0:r0 + HEAD_DIM, t0:t0 + W] = out_t[:, g * W:(g + 1) * W].astype(bf16)

    lane = lax.broadcasted_iota(jnp.int32, (1, LANES), 1)
    lane_even = (lane & 1) == 0
    cos_b = cosb_ref[pl.ds(blk, 1), :]
    sin_b = sinb_ref[pl.ds(blk, 1), :]
    cos_i, sin_i = cosi_ref[...], sini_ref[...]
    cos = cos_b * cos_i - sin_b * sin_i
    sin = sin_b * cos_i + cos_b * sin_i
    sin_signed = jnp.where(lane_even, -sin, sin)

    def rotate(t):
        swapped = jnp.where(lane_even, pltpu.roll(t, LANES - 1, 1), pltpu.roll(t, 1, 1))
        return t * cos + swapped * sin_signed

    def ret_front(hh):
        c0 = hh * RET_HEAD_DIM
        q = rotate(proj_s[:, S_QR + c0:S_QR + c0 + RET_HEAD_DIM])
        k = rotate(proj_s[:, S_KR + c0:S_KR + c0 + RET_HEAD_DIM])
        return q, k, _dot_nt(k.astype(bf16), q.astype(bf16))

    def ret_back(hh, q, k, kq_t):
        c0 = hh * RET_HEAD_DIM
        vt = proj_t[T_VR + c0:T_VR + c0 + RET_HEAD_DIM, :].astype(bf16)
        gate_t = proj_t[T_GR + c0:T_GR + c0 + RET_HEAD_DIM, :]
        inner_t = kq_t * decay_ref[hh]
        state_t = state_ref[hh]
        o_t = (_dot(vt, inner_t.astype(bf16))
               + _dot_nt(state_t.astype(bf16), (q * xi_ref[hh]).astype(bf16)))
        state_ref[hh] = chunk_decay[hh] * state_t + _dot(vt, (k * zeta_ref[hh]).astype(bf16))
        mu = jnp.mean(o_t, axis=0, keepdims=True)
        d = o_t - mu
        var = jnp.mean(d * d, axis=0, keepdims=True)
        res = d * lax.rsqrt(var + GN_EPS) * (gate_t * (1.0 / (1.0 + jnp.exp(-gate_t))))
        mixt_ref[ATTN_W + c0:ATTN_W + c0 + RET_HEAD_DIM, :] = res.astype(bf16)

    def ffn_epilogue():
        o_ref[...] = x1_ref[...] + _rms(o_ref[...], g4_ref[...])

    def next_input_norm():
        h_ref[...] = _rms(xn_ref[...], g1_ref[...]).astype(bf16)

    attn_ids = [(b, hk) for b in range(T // W) for hk in range(N_KV_HEADS)]
    pairs = ([(attn_front, attn_back, ids) for ids in (attn_ids[0:2], attn_ids[2:4])]
             + [(ret_front, ret_back, ids) for ids in ([(0,), (1,)], [(2,), (3,)])])
    for n, (front, back, ids) in enumerate(pairs):
        fronts = [front(*i) for i in ids]
        ffn_down_unit(n)
        for i, f in zip(ids, fronts):
            back(*i, *(f if isinstance(f, tuple) else (f,)))
        if n == len(pairs) - 2:
            next_input_norm()
    ffn_epilogue()

    x1_ref[...] = _dot_tn(mixt_ref[...], w_out_ref[...])


def kernel(x, mix_pre_norm, w_in, attn_sinks, w_out, mix_post_norm, ffn_pre_norm, w_up,
           conv_w, conv_b, w_down, ffn_post_norm):
    batch, seq, d_model = x.shape
    depth = w_in.shape[0]
    assert d_model == D_MODEL and seq % SEQ_BLOCK == 0
    tabs = _retention_tables(seq)
    n_steps = seq // SEQ_BLOCK
    bf16 = jnp.bfloat16

    def resident(shape):
        nd = len(shape)
        return pl.BlockSpec(shape, lambda i, _n=nd: (0,) * _n, pipeline_mode=pl.Buffered(1))

    in_specs = [
        pl.BlockSpec((SEQ_BLOCK, D_MODEL), lambda i: (jnp.maximum(i - 1, 0), 0)),
        pl.BlockSpec((SEQ_BLOCK, D_MODEL), lambda i: (jnp.minimum(i + 1, n_steps - 1), 0)),
        resident((1, D_MODEL)), resident((1, D_MODEL)), resident((1, D_MODEL)), resident((1, D_MODEL)),
        resident((N_KV_HEADS, Q_PER_KV * WINDOW)),
        pl.BlockSpec(memory_space=pl.ANY), pl.BlockSpec(memory_space=pl.ANY),
        pl.BlockSpec(memory_space=pl.ANY), pl.BlockSpec(memory_space=pl.ANY),
        resident((3, 2 * D_FF)), resident((1, 2 * D_FF)),
        resident((n_steps, LANES)), resident((n_steps, LANES)),
        resident((SEQ_BLOCK, LANES)), resident((SEQ_BLOCK, LANES)),
        resident((N_RET_HEADS, SEQ_BLOCK, SEQ_BLOCK)),
        resident((N_RET_HEADS, SEQ_BLOCK, LANES)), resident((N_RET_HEADS, SEQ_BLOCK, LANES)),
    ]
    n_conv_slabs = 2 * D_FF // LANES
    call = pl.pallas_call(
        functools.partial(_layer_kernel, tabs["chunk_decay"], n_steps),
        grid=(n_steps + 1,),
        in_specs=in_specs,
        out_specs=pl.BlockSpec((SEQ_BLOCK, D_MODEL), lambda i: (jnp.maximum(i - 1, 0), 0)),
        out_shape=jax.ShapeDtypeStruct((seq, D_MODEL), jnp.float32),
        scratch_shapes=[
            pltpu.VMEM((T_ROWS, D_MODEL), bf16),
            pltpu.VMEM((D_MODEL, S_COLS), bf16),
            pltpu.VMEM((D_MODEL, D_MODEL), bf16),
            pltpu.VMEM((D_MODEL, 2 * D_FF), bf16),
            pltpu.VMEM((D_FF, D_MODEL), bf16),
            pltpu.SemaphoreType.DMA((2 * n_conv_slabs,)),
            pltpu.VMEM((WINDOW, KV_W), jnp.float32),
            pltpu.VMEM((KV_W, WINDOW), jnp.float32),
            pltpu.VMEM((N_RET_HEADS, RET_HEAD_DIM, RET_HEAD_DIM), jnp.float32),
            pltpu.VMEM((n_conv_slabs, CONV_ROW0 + 2 * SEQ_BLOCK, LANES), jnp.float32),
            pltpu.VMEM((D_MODEL, SEQ_BLOCK), bf16),
            pltpu.VMEM((SEQ_BLOCK, D_FF), bf16),
            pltpu.VMEM((SEQ_BLOCK, D_MODEL), jnp.float32),
            pltpu.VMEM((SEQ_BLOCK, D_MODEL), bf16),
        ],
        compiler_params=pltpu.CompilerParams(
            dimension_semantics=("arbitrary",), vmem_limit_bytes=VMEM_LIMIT_BYTES),
        name="hybrid_layer",
    )

    outs = []
    for bi in range(batch):
        xb = x[bi]
        for l in range(depth):
            sink_rows = jnp.repeat(attn_sinks[l].reshape(N_KV_HEADS, Q_PER_KV), WINDOW, axis=1)
            xb = call(
                xb, xb,
                mix_pre_norm[l].reshape(1, D_MODEL), mix_post_norm[l].reshape(1, D_MODEL),
                ffn_pre_norm[l].reshape(1, D_MODEL), ffn_post_norm[l].reshape(1, D_MODEL),
                sink_rows,
                w_in[l], w_out[l], w_up[l], w_down[l],
                conv_w[l], conv_b[l].reshape(1, 2 * D_FF),
                tabs["cos_base"], tabs["sin_base"], tabs["cos_intra"], tabs["sin_intra"],
                tabs["decay_t"], tabs["xi"], tabs["zeta"],
            )
        outs.append(xb)
    return outs[0][None] if batch == 1 else jnp.stack(outs, axis=0)
```

```python
import functools

import numpy as np
import jax
import jax.numpy as jnp
from jax import lax
from jax.experimental import pallas as pl
from jax.experimental.pallas import tpu as pltpu

D_MODEL = 1024
HEAD_DIM = 64
ATTN_W = 512
N_ATTN_HEADS = 8
N_KV_HEADS = 2
Q_PER_KV = N_ATTN_HEADS // N_KV_HEADS
KV_W = N_KV_HEADS * HEAD_DIM
WINDOW = 128
RET_W = 512
N_RET_HEADS = 4
RET_HEAD_DIM = 128
D_FF = 2816
RMS_EPS = 1e-6
GN_EPS = 1e-6
MASK_VALUE = -1e30
GELU_C = 0.7978845608028654

LANES = 128
SUBLANES = 8
MXU_TILE = 256
SEQ_BLOCK = 256
FF_TILE = MXU_TILE
N_FF_TILES = D_FF // FF_TILE
CONV_ROW0 = 2 * SUBLANES
STAGE_ROWS = SEQ_BLOCK
V7X_VMEM_BYTES = 64 * 1024 * 1024
VMEM_LIMIT_BYTES = V7X_VMEM_BYTES - 4 * 1024 * 1024

T_QA, T_KA, T_VA, T_VR, T_GR = 0, ATTN_W, ATTN_W + KV_W, ATTN_W + 2 * KV_W, ATTN_W + 2 * KV_W + RET_W
T_ROWS = ATTN_W + 2 * KV_W + 2 * RET_W
S_QR, S_KR = 0, RET_W
S_COLS = 2 * RET_W


def _retention_tables(seq):
    c = SEQ_BLOCK
    n_steps = seq // c
    angle = 1.0 / np.power(10000.0, np.linspace(0.0, 1.0, RET_HEAD_DIM // 2))
    angle = np.repeat(angle, 2)
    base = (np.arange(n_steps, dtype=np.float64) * c)[:, None] * angle[None]
    intra = np.arange(c, dtype=np.float64)[:, None] * angle[None]
    gamma = 1.0 - np.power(2.0, -5.0 - np.arange(N_RET_HEADS, dtype=np.float64))
    k_scale = RET_HEAD_DIM ** -0.5
    idx = np.arange(c, dtype=np.float64)
    rel = idx[None, :] - idx[:, None]
    decay_t = np.where(rel[None] >= 0, gamma[:, None, None] ** np.maximum(rel, 0.0)[None], 0.0)
    xi = gamma[:, None] ** (idx[None, :] + 1.0)
    zeta = gamma[:, None] ** (c - 1.0 - idx[None, :])
    f32 = lambda a: jnp.asarray(a, dtype=jnp.float32)
    return dict(
        cos_base=f32(np.cos(base)), sin_base=f32(np.sin(base)),
        cos_intra=f32(np.cos(intra)), sin_intra=f32(np.sin(intra)),
        decay_t=f32(k_scale * decay_t),
        xi=f32(np.broadcast_to(xi[:, :, None], (N_RET_HEADS, c, LANES))),
        zeta=f32(np.broadcast_to(k_scale * zeta[:, :, None], (N_RET_HEADS, c, LANES))),
        chunk_decay=[float(g ** c) for g in gamma],
    )


def _rms(x, w):
    return x * lax.rsqrt(jnp.mean(x * x, axis=-1, keepdims=True) + RMS_EPS) * w


def _dot(a, b):
    return jnp.dot(a, b, preferred_element_type=jnp.float32)


def _dot_nt(a, b):
    return lax.dot_general(a, b, (((1,), (1,)), ((), ())), preferred_element_type=jnp.float32)


def _dot_tn(a, b):
    return lax.dot_general(a, b, (((0,), (0,)), ((), ())), preferred_element_type=jnp.float32)


def _up_col(c):
    part, c = divmod(c, D_FF)
    return (c // FF_TILE) * 2 * FF_TILE + part * FF_TILE + c % FF_TILE


def _stage_weights(w_in_hbm, w_out_hbm, w_up_hbm, w_down_hbm,
                   w_t_ref, w_s_ref, w_out_ref, w_up_ref, w_down_ref, stage_ref, sem):
    bf16 = jnp.bfloat16
    R = STAGE_ROWS

    def plain(dst_ref, dst_col, scale=None):
        def store(tile, row0):
            dst_ref[row0:row0 + R, dst_col:dst_col + LANES] = (tile if scale is None else tile * scale).astype(bf16)
        return store

    def transposed(dst_row, scale=None):
        def store(tile, row0):
            tile = tile if scale is None else tile * scale
            w_t_ref[dst_row:dst_row + LANES, row0:row0 + R] = jnp.transpose(tile).astype(bf16)
        return store

    in_slabs = ([transposed(T_QA + i * LANES, HEAD_DIM ** -0.5) for i in range(ATTN_W // LANES)]
                + [transposed(T_KA), transposed(T_VA)]
                + [plain(w_s_ref, S_QR + i * LANES) for i in range(RET_W // LANES)]
                + [plain(w_s_ref, S_KR + i * LANES) for i in range(RET_W // LANES)]
                + [transposed(T_VR + i * LANES) for i in range(RET_W // LANES)]
                + [transposed(T_GR + i * LANES) for i in range(RET_W // LANES)])
    tiles = []
    for src, stores in ((w_in_hbm, in_slabs),
                        (w_out_hbm, [plain(w_out_ref, c) for c in range(0, D_MODEL, LANES)]),
                        (w_up_hbm, [plain(w_up_ref, _up_col(c)) for c in range(0, 2 * D_FF, LANES)]),
                        (w_down_hbm, [plain(w_down_ref, c, 0.5) for c in range(0, D_MODEL, LANES)])):
        for row0 in range(0, src.shape[0], R):
            tiles += [(src, row0, i * LANES, store) for i, store in enumerate(stores)]

    n_slabs = stage_ref.shape[0]
    n_slots = 2 * n_slabs

    def copy(t):
        src, row0, col0, _ = tiles[t]
        slot = t % n_slots
        return pltpu.make_async_copy(src.at[pl.ds(row0, R), pl.ds(col0, LANES)],
                                     stage_ref.at[slot % n_slabs, pl.ds((slot // n_slabs) * R, R), :],
                                     sem.at[slot])

    for t in range(min(n_slots, len(tiles))):
        copy(t).start()
    for t, (_, row0, _, store) in enumerate(tiles):
        copy(t).wait()
        slot = t % n_slots
        half = slot // n_slabs
        store(stage_ref[slot % n_slabs, half * R:(half + 1) * R, :], row0)
        if t + n_slots < len(tiles):
            copy(t + n_slots).start()


def _layer_kernel(chunk_decay, n_blocks,
                  xp_ref, xn_ref, g1_ref, g2_ref, g3_ref, g4_ref, sink_ref,
                  w_in_hbm, w_out_hbm, w_up_hbm, w_down_hbm, cw_ref, cb_ref,
                  cosb_ref, sinb_ref, cosi_ref, sini_ref, decay_ref, xi_ref, zeta_ref,
                  o_ref,
                  w_t_ref, w_s_ref, w_out_ref, w_up_ref, w_down_ref, stage_sem,
                  kprev_ref, vtprev_ref, state_ref, pbuf_ref, mixt_ref, y_ref, x1_ref, h_ref):
    step = pl.program_id(0)
    bf16 = jnp.bfloat16
    T = SEQ_BLOCK
    W = WINDOW

    @pl.when(step == 0)
    def _init():
        _stage_weights(w_in_hbm, w_out_hbm, w_up_hbm, w_down_hbm,
                       w_t_ref, w_s_ref, w_out_ref, w_up_ref, w_down_ref, pbuf_ref, stage_sem)
        kprev_ref[...] = jnp.zeros_like(kprev_ref)
        vtprev_ref[...] = jnp.zeros_like(vtprev_ref)
        state_ref[...] = jnp.zeros_like(state_ref)
        pbuf_ref[...] = jnp.zeros_like(pbuf_ref)
        x1_ref[...] = jnp.zeros_like(x1_ref)
        h_ref[...] = _rms(xp_ref[...], g1_ref[...]).astype(bf16)

    blk = jnp.minimum(step, n_blocks - 1)
    prev_valid = jnp.where(step > 0, 1.0, 0.0)

    h = h_ref[...]
    proj_t = _dot_nt(w_t_ref[...], h)
    proj_s = _dot(h, w_s_ref[...])
    x1 = (xp_ref[...] + _rms(x1_ref[...], g2_ref[...])) * prev_valid
    x1_ref[...] = x1
    h2 = _rms(x1, g3_ref[...]).astype(bf16)

    def conv(p, slab, c0):
        pbuf_ref[slab, pl.ds(CONV_ROW0, T, stride=2), :] = p
        p1 = pbuf_ref[slab, pl.ds(CONV_ROW0 - 2, T, stride=2), :]
        p2 = pbuf_ref[slab, pl.ds(CONV_ROW0 - 4, T, stride=2), :]
        pbuf_ref[slab, CONV_ROW0 - 4:CONV_ROW0, :] = pbuf_ref[slab, CONV_ROW0 + 2 * T - 4:CONV_ROW0 + 2 * T, :]
        return (cw_ref[2:3, c0:c0 + LANES] * p + cw_ref[1:2, c0:c0 + LANES] * p1
                + cw_ref[0:1, c0:c0 + LANES] * p2 + cb_ref[:, c0:c0 + LANES])

    def ffn_up_unit(j):
        c_gate, c_val = j * FF_TILE, D_FF + j * FF_TILE
        p = _dot(h2, w_up_ref[:, 2 * c_gate:2 * c_gate + 2 * FF_TILE])
        p_gate, p_val = p[:, :FF_TILE], p[:, FF_TILE:]
        for half in range(FF_TILE // LANES):
            l0 = half * LANES
            ug = conv(p_gate[:, l0:l0 + LANES], (c_gate + l0) // LANES, c_gate + l0)
            uv = conv(p_val[:, l0:l0 + LANES], (c_val + l0) // LANES, c_val + l0)
            t = jnp.tanh(ug * (ug * ug * (GELU_C * 0.044715) + GELU_C))
            y_ref[:, c_gate + l0:c_gate + l0 + LANES] = (ug * (1.0 + t) * uv).astype(bf16)

    def ffn_down_unit(n):
        c0 = n * MXU_TILE
        o_ref[:, c0:c0 + MXU_TILE] = _dot(y_ref[...], w_down_ref[:, c0:c0 + MXU_TILE])

    for j in range(N_FF_TILES - 1):
        ffn_up_unit(j)

    key_idx = lax.broadcasted_iota(jnp.int32, (W, Q_PER_KV * W), 0)
    qry_idx = lax.broadcasted_iota(jnp.int32, (W, Q_PER_KV * W), 1) & (W - 1)
    from_prev = key_idx > qry_idx
    first_block_penalty = jnp.where(step == 0, MASK_VALUE, 0.0)
    k_a = jnp.transpose(proj_t[T_KA:T_KA + KV_W, :])
    vt_a = proj_t[T_VA:T_VA + KV_W, :]
    zeros_half = jnp.zeros((HEAD_DIM, Q_PER_KV * W), bf16)
    kk_first = jnp.concatenate([kprev_ref[...], k_a[:W]], axis=0)
    vvt_first = jnp.concatenate([vtprev_ref[...], vt_a[:, :W]], axis=1)
    kprev_ref[...] = k_a[T - W:]
    vtprev_ref[...] = vt_a[:, T - W:]

    def attn_front(b, hk):
        t0 = b * W
        kk = (kk_first if b == 0 else k_a[t0 - W:t0 + W]).astype(bf16)
        h0 = hk * Q_PER_KV
        qt = jnp.concatenate(
            [proj_t[T_QA + (h0 + g) * HEAD_DIM:T_QA + (h0 + g + 1) * HEAD_DIM, t0:t0 + W]
             for g in range(Q_PER_KV)], axis=1).astype(bf16)
        rhs = jnp.concatenate([qt, zeros_half] if hk == 0 else [zeros_half, qt], axis=0)
        return _dot(kk, rhs)

    def attn_back(b, hk, st):
        t0 = b * W
        vvt = (vvt_first if b == 0 else vt_a[:, t0 - W:t0 + W]).astype(bf16)
        h0 = hk * Q_PER_KV
        s_prev = st[:W] + first_block_penalty if b == 0 else st[:W]
        s = jnp.where(from_prev, s_prev, st[W:])
        sink = sink_ref[hk:hk + 1, :]
        m = jnp.maximum(jnp.max(s, axis=0, keepdims=True), sink)
        p = jnp.exp(s - m)
        denom = jnp.sum(p, axis=0, keepdims=True) + jnp.exp(sink - m)
        pt = jnp.concatenate([jnp.where(from_prev, p, 0.0).astype(bf16),
                              jnp.where(from_prev, 0.0, p).astype(bf16)], axis=0)
        out_t = _dot(vvt[hk * HEAD_DIM:(hk + 1) * HEAD_DIM, :], pt) * (1.0 / denom)
        for g in range(Q_PER_KV):
            r0 = (h0 + g) * HEAD_DIM
            mixt_ref[r0:r0 + HEAD_DIM, t0:t0 + W] = out_t[:, g * W:(g + 1) * W].astype(bf16)

    lane = lax.broadcasted_iota(jnp.int32, (1, LANES), 1)
    lane_even = (lane & 1) == 0
    cos_b = cosb_ref[pl.ds(blk, 1), :]
    sin_b = sinb_ref[pl.ds(blk, 1), :]
    cos_i, sin_i = cosi_ref[...], sini_ref[...]
    cos = cos_b * cos_i - sin_b * sin_i
    sin = sin_b * cos_i + cos_b * sin_i
    sin_signed = jnp.where(lane_even, -sin, sin)

    def rotate(t):
        swapped = jnp.where(lane_even, pltpu.roll(t, LANES - 1, 1), pltpu.roll(t, 1, 1))
        return t * cos + swapped * sin_signed

    def ret_front(hh):
        c0 = hh * RET_HEAD_DIM
        q = rotate(proj_s[:, S_QR + c0:S_QR + c0 + RET_HEAD_DIM])
        k = rotate(proj_s[:, S_KR + c0:S_KR + c0 + RET_HEAD_DIM])
        return q, k, _dot_nt(k.astype(bf16), q.astype(bf16))

    def ret_back(hh, q, k, kq_t):
        c0 = hh * RET_HEAD_DIM
        vt = proj_t[T_VR + c0:T_VR + c0 + RET_HEAD_DIM, :].astype(bf16)
        gate_t = proj_t[T_GR + c0:T_GR + c0 + RET_HEAD_DIM, :]
        inner_t = kq_t * decay_ref[hh]
        state_t = state_ref[hh]
        o_t = (_dot(vt, inner_t.astype(bf16))
               + _dot_nt(state_t.astype(bf16), (q * xi_ref[hh]).astype(bf16)))
        state_ref[hh] = chunk_decay[hh] * state_t + _dot(vt, (k * zeta_ref[hh]).astype(bf16))
        mu = jnp.mean(o_t, axis=0, keepdims=True)
        d = o_t - mu
        var = jnp.mean(d * d, axis=0, keepdims=True)
        res = d * lax.rsqrt(var + GN_EPS) * (gate_t * (1.0 / (1.0 + jnp.exp(-gate_t))))
        mixt_ref[ATTN_W + c0:ATTN_W + c0 + RET_HEAD_DIM, :] = res.astype(bf16)

    def ffn_epilogue():
        o_ref[...] = x1_ref[...] + _rms(o_ref[...], g4_ref[...])

    def next_input_norm():
        h_ref[...] = _rms(xn_ref[...], g1_ref[...]).astype(bf16)

    attn_ids = [(b, hk) for b in range(T // W) for hk in range(N_KV_HEADS)]
    pairs = ([(attn_front, attn_back, ids) for ids in (attn_ids[0:2], attn_ids[2:4])]
             + [(ret_front, ret_back, ids) for ids in ([(0,), (1,)], [(2,), (3,)])])
    for n, (front, back, ids) in enumerate(pairs):
        fronts = [front(*i) for i in ids]
        if n == 0:
            ffn_up_unit(N_FF_TILES - 1)
        ffn_down_unit(n)
        for i, f in zip(ids, fronts):
            back(*i, *(f if isinstance(f, tuple) else (f,)))
        if n == len(pairs) - 2:
            next_input_norm()
    ffn_epilogue()

    x1_ref[...] = _dot_tn(mixt_ref[...], w_out_ref[...])


def kernel(x, mix_pre_norm, w_in, attn_sinks, w_out, mix_post_norm, ffn_pre_norm, w_up,
           conv_w, conv_b, w_down, ffn_post_norm):
    batch, seq, d_model = x.shape
    depth = w_in.shape[0]
    assert d_model == D_MODEL and seq % SEQ_BLOCK == 0
    tabs = _retention_tables(seq)
    n_steps = seq // SEQ_BLOCK
    bf16 = jnp.bfloat16

    def resident(shape):
        nd = len(shape)
        return pl.BlockSpec(shape, lambda i, _n=nd: (0,) * _n, pipeline_mode=pl.Buffered(1))

    in_specs = [
        pl.BlockSpec((SEQ_BLOCK, D_MODEL), lambda i: (jnp.maximum(i - 1, 0), 0)),
        pl.BlockSpec((SEQ_BLOCK, D_MODEL), lambda i: (jnp.minimum(i + 1, n_steps - 1), 0)),
        resident((1, D_MODEL)), resident((1, D_MODEL)), resident((1, D_MODEL)), resident((1, D_MODEL)),
        resident((N_KV_HEADS, Q_PER_KV * WINDOW)),
        pl.BlockSpec(memory_space=pl.ANY), pl.BlockSpec(memory_space=pl.ANY),
        pl.BlockSpec(memory_space=pl.ANY), pl.BlockSpec(memory_space=pl.ANY),
        resident((3, 2 * D_FF)), resident((1, 2 * D_FF)),
        resident((n_steps, LANES)), resident((n_steps, LANES)),
        resident((SEQ_BLOCK, LANES)), resident((SEQ_BLOCK, LANES)),
        resident((N_RET_HEADS, SEQ_BLOCK, SEQ_BLOCK)),
        resident((N_RET_HEADS, SEQ_BLOCK, LANES)), resident((N_RET_HEADS, SEQ_BLOCK, LANES)),
    ]
    n_conv_slabs = 2 * D_FF // LANES
    call = pl.pallas_call(
        functools.partial(_layer_kernel, tabs["chunk_decay"], n_steps),
        grid=(n_steps + 1,),
        in_specs=in_specs,
        out_specs=pl.BlockSpec((SEQ_BLOCK, D_MODEL), lambda i: (jnp.maximum(i - 1, 0), 0)),
        out_shape=jax.ShapeDtypeStruct((seq, D_MODEL), jnp.float32),
        scratch_shapes=[
            pltpu.VMEM((T_ROWS, D_MODEL), bf16),
            pltpu.VMEM((D_MODEL, S_COLS), bf16),
            pltpu.VMEM((D_MODEL, D_MODEL), bf16),
            pltpu.VMEM((D_MODEL, 2 * D_FF), bf16),
            pltpu.VMEM((D_FF, D_MODEL), bf16),
            pltpu.SemaphoreType.DMA((2 * n_conv_slabs,)),
            pltpu.VMEM((WINDOW, KV_W), jnp.float32),
            pltpu.VMEM((KV_W, WINDOW), jnp.float32),
            pltpu.VMEM((N_RET_HEADS, RET_HEAD_DIM, RET_HEAD_DIM), jnp.float32),
            pltpu.VMEM((n_conv_slabs, CONV_ROW0 + 2 * SEQ_BLOCK, LANES), jnp.float32),
            pltpu.VMEM((D_MODEL, SEQ_BLOCK), bf16),
            pltpu.VMEM((SEQ_BLOCK, D_FF), bf16),
            pltpu.VMEM((SEQ_BLOCK, D_MODEL), jnp.float32),
            pltpu.VMEM((SEQ_BLOCK, D_MODEL), bf16),
        ],
        compiler_params=pltpu.CompilerParams(
            dimension_semantics=("arbitrary",), vmem_limit_bytes=VMEM_LIMIT_BYTES),
        name="hybrid_layer",
    )

    outs = []
    for bi in range(batch):
        xb = x[bi]
        for l in range(depth):
            sink_rows = jnp.repeat(attn_sinks[l].reshape(N_KV_HEADS, Q_PER_KV), WINDOW, axis=1)
            xb = call(
                xb, xb,
                mix_pre_norm[l].reshape(1, D_MODEL), mix_post_norm[l].reshape(1, D_MODEL),
                ffn_pre_norm[l].reshape(1, D_MODEL), ffn_post_norm[l].reshape(1, D_MODEL),
                sink_rows,
                w_in[l], w_out[l], w_up[l], w_down[l],
                conv_w[l], conv_b[l].reshape(1, 2 * D_FF),
                tabs["cos_base"], tabs["sin_base"], tabs["cos_intra"], tabs["sin_intra"],
                tabs["decay_t"], tabs["xi"], tabs["zeta"],
            )
        outs.append(xb)
    return outs[0][None] if batch == 1 else jnp.stack(outs, axis=0)
```

```python
import functools

import numpy as np
import jax
import jax.numpy as jnp
from jax import lax
from jax.experimental import pallas as pl
from jax.experimental.pallas import tpu as pltpu

D_MODEL = 1024
HEAD_DIM = 64
ATTN_W = 512
N_ATTN_HEADS = 8
N_KV_HEADS = 2
Q_PER_KV = N_ATTN_HEADS // N_KV_HEADS
KV_W = N_KV_HEADS * HEAD_DIM
WINDOW = 128
RET_W = 512
N_RET_HEADS = 4
RET_HEAD_DIM = 128
D_FF = 2816
RMS_EPS = 1e-6
GN_EPS = 1e-6
MASK_VALUE = -1e30
GELU_C = 0.7978845608028654

LANES = 128
SUBLANES = 8
MXU_TILE = 256
SEQ_BLOCK = 256
FF_TILE = MXU_TILE
N_FF_TILES = D_FF // FF_TILE
CONV_ROW0 = 2 * SUBLANES
STAGE_ROWS = SEQ_BLOCK
V7X_VMEM_BYTES = 64 * 1024 * 1024
VMEM_LIMIT_BYTES = V7X_VMEM_BYTES - 4 * 1024 * 1024

T_QA, T_KA, T_VA, T_VR, T_GR = 0, ATTN_W, ATTN_W + KV_W, ATTN_W + 2 * KV_W, ATTN_W + 2 * KV_W + RET_W
T_ROWS = ATTN_W + 2 * KV_W + 2 * RET_W
S_QR, S_KR = 0, RET_W
S_COLS = 2 * RET_W


def _retention_tables(seq):
    c = SEQ_BLOCK
    n_steps = seq // c
    angle = 1.0 / np.power(10000.0, np.linspace(0.0, 1.0, RET_HEAD_DIM // 2))
    angle = np.repeat(angle, 2)
    base = (np.arange(n_steps, dtype=np.float64) * c)[:, None] * angle[None]
    intra = np.arange(c, dtype=np.float64)[:, None] * angle[None]
    gamma = 1.0 - np.power(2.0, -5.0 - np.arange(N_RET_HEADS, dtype=np.float64))
    k_scale = RET_HEAD_DIM ** -0.5
    idx = np.arange(c, dtype=np.float64)
    rel = idx[None, :] - idx[:, None]
    decay_t = np.where(rel[None] >= 0, gamma[:, None, None] ** np.maximum(rel, 0.0)[None], 0.0)
    xi = gamma[:, None] ** (idx[None, :] + 1.0)
    zeta = gamma[:, None] ** (c - 1.0 - idx[None, :])
    f32 = lambda a: jnp.asarray(a, dtype=jnp.float32)
    return dict(
        cos_base=f32(np.cos(base)), sin_base=f32(np.sin(base)),
        cos_intra=f32(np.cos(intra)), sin_intra=f32(np.sin(intra)),
        decay_t=f32(k_scale * decay_t),
        xi=f32(np.broadcast_to(xi[:, :, None], (N_RET_HEADS, c, LANES))),
        zeta=f32(np.broadcast_to(k_scale * zeta[:, :, None], (N_RET_HEADS, c, LANES))),
        chunk_decay=[float(g ** c) for g in gamma],
    )


def _rms(x, w):
    return x * lax.rsqrt(jnp.mean(x * x, axis=-1, keepdims=True) + RMS_EPS) * w


def _dot(a, b):
    return jnp.dot(a, b, preferred_element_type=jnp.float32)


def _dot_nt(a, b):
    return lax.dot_general(a, b, (((1,), (1,)), ((), ())), preferred_element_type=jnp.float32)


def _dot_tn(a, b):
    return lax.dot_general(a, b, (((0,), (0,)), ((), ())), preferred_element_type=jnp.float32)


def _up_col(c):
    part, c = divmod(c, D_FF)
    return (c // FF_TILE) * 2 * FF_TILE + part * FF_TILE + c % FF_TILE


def _stage_weights(w_in_hbm, w_out_hbm, w_up_hbm, w_down_hbm,
                   w_t_ref, w_s_ref, w_out_ref, w_up_ref, w_down_ref, stage_ref, sem):
    bf16 = jnp.bfloat16
    R = STAGE_ROWS

    def plain(dst_ref, dst_col, scale=None):
        def store(tile, row0):
            dst_ref[row0:row0 + R, dst_col:dst_col + LANES] = (tile if scale is None else tile * scale).astype(bf16)
        return store

    def transposed(dst_row, scale=None):
        def store(tile, row0):
            tile = tile if scale is None else tile * scale
            w_t_ref[dst_row:dst_row + LANES, row0:row0 + R] = jnp.transpose(tile).astype(bf16)
        return store

    in_slabs = ([transposed(T_QA + i * LANES, HEAD_DIM ** -0.5) for i in range(ATTN_W // LANES)]
                + [transposed(T_KA), transposed(T_VA)]
                + [plain(w_s_ref, S_QR + i * LANES) for i in range(RET_W // LANES)]
                + [plain(w_s_ref, S_KR + i * LANES) for i in range(RET_W // LANES)]
                + [transposed(T_VR + i * LANES) for i in range(RET_W // LANES)]
                + [transposed(T_GR + i * LANES) for i in range(RET_W // LANES)])
    tiles = []
    for src, stores in ((w_in_hbm, in_slabs),
                        (w_out_hbm, [plain(w_out_ref, c) for c in range(0, D_MODEL, LANES)]),
                        (w_up_hbm, [plain(w_up_ref, _up_col(c)) for c in range(0, 2 * D_FF, LANES)]),
                        (w_down_hbm, [plain(w_down_ref, c, 0.5) for c in range(0, D_MODEL, LANES)])):
        for row0 in range(0, src.shape[0], R):
            tiles += [(src, row0, i * LANES, store) for i, store in enumerate(stores)]

    n_slabs = stage_ref.shape[0]
    n_slots = 2 * n_slabs

    def copy(t):
        src, row0, col0, _ = tiles[t]
        slot = t % n_slots
        return pltpu.make_async_copy(src.at[pl.ds(row0, R), pl.ds(col0, LANES)],
                                     stage_ref.at[slot % n_slabs, pl.ds((slot // n_slabs) * R, R), :],
                                     sem.at[slot])

    for t in range(min(n_slots, len(tiles))):
        copy(t).start()
    for t, (_, row0, _, store) in enumerate(tiles):
        copy(t).wait()
        slot = t % n_slots
        half = slot // n_slabs
        store(stage_ref[slot % n_slabs, half * R:(half + 1) * R, :], row0)
        if t + n_slots < len(tiles):
            copy(t + n_slots).start()


def _layer_kernel(chunk_decay, n_blocks,
                  xp_ref, xn_ref, g1_ref, g2_ref, g3_ref, g4_ref, sink_ref,
                  w_in_hbm, w_out_hbm, w_up_hbm, w_down_hbm, cw_ref, cb_ref,
                  cosb_ref, sinb_ref, cosi_ref, sini_ref, decay_ref, xi_ref, zeta_ref,
                  o_ref,
                  w_t_ref, w_s_ref, w_out_ref, w_up_ref, w_down_ref, stage_sem,
                  kprev_ref, vtprev_ref, state_ref, pbuf_ref, mixt_ref, y_ref, x1_ref, h_ref):
    step = pl.program_id(0)
    bf16 = jnp.bfloat16
    T = SEQ_BLOCK
    W = WINDOW

    @pl.when(step == 0)
    def _init():
        _stage_weights(w_in_hbm, w_out_hbm, w_up_hbm, w_down_hbm,
                       w_t_ref, w_s_ref, w_out_ref, w_up_ref, w_down_ref, pbuf_ref, stage_sem)
        kprev_ref[...] = jnp.zeros_like(kprev_ref)
        vtprev_ref[...] = jnp.zeros_like(vtprev_ref)
        state_ref[...] = jnp.zeros_like(state_ref)
        pbuf_ref[...] = jnp.zeros_like(pbuf_ref)
        x1_ref[...] = jnp.zeros_like(x1_ref)
        h_ref[...] = _rms(xp_ref[...], g1_ref[...]).astype(bf16)

    blk = jnp.minimum(step, n_blocks - 1)
    prev_valid = jnp.where(step > 0, 1.0, 0.0)

    h = h_ref[...]
    proj_t = _dot_nt(w_t_ref[...], h)
    proj_s = _dot(h, w_s_ref[...])
    x1 = (xp_ref[...] + _rms(x1_ref[...], g2_ref[...])) * prev_valid
    x1_ref[...] = x1
    h2 = _rms(x1, g3_ref[...]).astype(bf16)

    def conv(p, slab, c0):
        pbuf_ref[slab, pl.ds(CONV_ROW0, T, stride=2), :] = p
        p1 = pbuf_ref[slab, pl.ds(CONV_ROW0 - 2, T, stride=2), :]
        p2 = pbuf_ref[slab, pl.ds(CONV_ROW0 - 4, T, stride=2), :]
        pbuf_ref[slab, CONV_ROW0 - 4:CONV_ROW0, :] = pbuf_ref[slab, CONV_ROW0 + 2 * T - 4:CONV_ROW0 + 2 * T, :]
        return (cw_ref[2:3, c0:c0 + LANES] * p + cw_ref[1:2, c0:c0 + LANES] * p1
                + cw_ref[0:1, c0:c0 + LANES] * p2 + cb_ref[:, c0:c0 + LANES])

    def ffn_up_unit(j):
        c_gate, c_val = j * FF_TILE, D_FF + j * FF_TILE
        p = _dot(h2, w_up_ref[:, 2 * c_gate:2 * c_gate + 2 * FF_TILE])
        p_gate, p_val = p[:, :FF_TILE], p[:, FF_TILE:]
        for half in range(FF_TILE // LANES):
            l0 = half * LANES
            ug = conv(p_gate[:, l0:l0 + LANES], (c_gate + l0) // LANES, c_gate + l0)
            uv = conv(p_val[:, l0:l0 + LANES], (c_val + l0) // LANES, c_val + l0)
            t = jnp.tanh(ug * (ug * ug * (GELU_C * 0.044715) + GELU_C))
            y_ref[:, c_gate + l0:c_gate + l0 + LANES] = (ug * (1.0 + t) * uv).astype(bf16)

    def ffn_down_unit(n):
        c0 = n * MXU_TILE
        o_ref[:, c0:c0 + MXU_TILE] = _dot(y_ref[...], w_down_ref[:, c0:c0 + MXU_TILE])

    for j in range(N_FF_TILES - 2):
        ffn_up_unit(j)

    key_idx = lax.broadcasted_iota(jnp.int32, (W, Q_PER_KV * W), 0)
    qry_idx = lax.broadcasted_iota(jnp.int32, (W, Q_PER_KV * W), 1) & (W - 1)
    from_prev = key_idx > qry_idx
    first_block_penalty = jnp.where(step == 0, MASK_VALUE, 0.0)
    k_a = jnp.transpose(proj_t[T_KA:T_KA + KV_W, :])
    vt_a = proj_t[T_VA:T_VA + KV_W, :]
    zeros_half = jnp.zeros((HEAD_DIM, Q_PER_KV * W), bf16)
    kk_first = jnp.concatenate([kprev_ref[...], k_a[:W]], axis=0)
    vvt_first = jnp.concatenate([vtprev_ref[...], vt_a[:, :W]], axis=1)
    kprev_ref[...] = k_a[T - W:]
    vtprev_ref[...] = vt_a[:, T - W:]

    def attn_front(b, hk):
        t0 = b * W
        kk = (kk_first if b == 0 else k_a[t0 - W:t0 + W]).astype(bf16)
        h0 = hk * Q_PER_KV
        qt = jnp.concatenate(
            [proj_t[T_QA + (h0 + g) * HEAD_DIM:T_QA + (h0 + g + 1) * HEAD_DIM, t0:t0 + W]
             for g in range(Q_PER_KV)], axis=1).astype(bf16)
        rhs = jnp.concatenate([qt, zeros_half] if hk == 0 else [zeros_half, qt], axis=0)
        return _dot(kk, rhs)

    def attn_back(b, hk, st):
        t0 = b * W
        vvt = (vvt_first if b == 0 else vt_a[:, t0 - W:t0 + W]).astype(bf16)
        h0 = hk * Q_PER_KV
        s_prev = st[:W] + first_block_penalty if b == 0 else st[:W]
        s = jnp.where(from_prev, s_prev, st[W:])
        sink = sink_ref[hk:hk + 1, :]
        m = jnp.maximum(jnp.max(s, axis=0, keepdims=True), sink)
        p = jnp.exp(s - m)
        denom = jnp.sum(p, axis=0, keepdims=True) + jnp.exp(sink - m)
        pt = jnp.concatenate([jnp.where(from_prev, p, 0.0).astype(bf16),
                              jnp.where(from_prev, 0.0, p).astype(bf16)], axis=0)
        out_t = _dot(vvt[hk * HEAD_DIM:(hk + 1) * HEAD_DIM, :], pt) * (1.0 / denom)
        for g in range(Q_PER_KV):
            r0 = (h0 + g) * HEAD_DIM
            mixt_ref[r0:r0 + HEAD_DIM, t0:t0 + W] = out_t[:, g * W:(g + 1) * W].astype(bf16)

    lane = lax.broadcasted_iota(jnp.int32, (1, LANES), 1)
    lane_even = (lane & 1) == 0
    cos_b = cosb_ref[pl.ds(blk, 1), :]
    sin_b = sinb_ref[pl.ds(blk, 1), :]
    cos_i, sin_i = cosi_ref[...], sini_ref[...]
    cos = cos_b * cos_i - sin_b * sin_i
    sin = sin_b * cos_i + cos_b * sin_i
    sin_signed = jnp.where(lane_even, -sin, sin)

    def rotate(t):
        swapped = jnp.where(lane_even, pltpu.roll(t, LANES - 1, 1), pltpu.roll(t, 1, 1))
        return t * cos + swapped * sin_signed

    def ret_front(hh):
        c0 = hh * RET_HEAD_DIM
        q = rotate(proj_s[:, S_QR + c0:S_QR + c0 + RET_HEAD_DIM])
        k = rotate(proj_s[:, S_KR + c0:S_KR + c0 + RET_HEAD_DIM])
        return q, k, _dot_nt(k.astype(bf16), q.astype(bf16))

    def ret_back(hh, q, k, kq_t):
        c0 = hh * RET_HEAD_DIM
        vt = proj_t[T_VR + c0:T_VR + c0 + RET_HEAD_DIM, :].astype(bf16)
        gate_t = proj_t[T_GR + c0:T_GR + c0 + RET_HEAD_DIM, :]
        inner_t = kq_t * decay_ref[hh]
        state_t = state_ref[hh]
        o_t = (_dot(vt, inner_t.astype(bf16))
               + _dot_nt(state_t.astype(bf16), (q * xi_ref[hh]).astype(bf16)))
        state_ref[hh] = chunk_decay[hh] * state_t + _dot(vt, (k * zeta_ref[hh]).astype(bf16))
        mu = jnp.mean(o_t, axis=0, keepdims=True)
        d = o_t - mu
        var = jnp.mean(d * d, axis=0, keepdims=True)
        res = d * lax.rsqrt(var + GN_EPS) * (gate_t * (1.0 / (1.0 + jnp.exp(-gate_t))))
        mixt_ref[ATTN_W + c0:ATTN_W + c0 + RET_HEAD_DIM, :] = res.astype(bf16)

    def ffn_epilogue():
        o_ref[...] = x1_ref[...] + _rms(o_ref[...], g4_ref[...])

    def next_input_norm():
        h_ref[...] = _rms(xn_ref[...], g1_ref[...]).astype(bf16)

    attn_ids = [(b, hk) for b in range(T // W) for hk in range(N_KV_HEADS)]
    pairs = ([(attn_front, attn_back, ids) for ids in (attn_ids[0:2], attn_ids[2:4])]
             + [(ret_front, ret_back, ids) for ids in ([(0,), (1,)], [(2,), (3,)])])
    for n, (front, back, ids) in enumerate(pairs):
        fronts = [front(*i) for i in ids]
        if n == 0:
            ffn_up_unit(N_FF_TILES - 2)
            ffn_up_unit(N_FF_TILES - 1)
        ffn_down_unit(n)
        for i, f in zip(ids, fronts):
            back(*i, *(f if isinstance(f, tuple) else (f,)))
        if n == len(pairs) - 2:
            next_input_norm()
    ffn_epilogue()

    x1_ref[...] = _dot_tn(mixt_ref[...], w_out_ref[...])


def kernel(x, mix_pre_norm, w_in, attn_sinks, w_out, mix_post_norm, ffn_pre_norm, w_up,
           conv_w, conv_b, w_down, ffn_post_norm):
    batch, seq, d_model = x.shape
    depth = w_in.shape[0]
    assert d_model == D_MODEL and seq % SEQ_BLOCK == 0
    tabs = _retention_tables(seq)
    n_steps = seq // SEQ_BLOCK
    bf16 = jnp.bfloat16

    def resident(shape):
        nd = len(shape)
        return pl.BlockSpec(shape, lambda i, _n=nd: (0,) * _n, pipeline_mode=pl.Buffered(1))

    in_specs = [
        pl.BlockSpec((SEQ_BLOCK, D_MODEL), lambda i: (jnp.maximum(i - 1, 0), 0)),
        pl.BlockSpec((SEQ_BLOCK, D_MODEL), lambda i: (jnp.minimum(i + 1, n_steps - 1), 0)),
        resident((1, D_MODEL)), resident((1, D_MODEL)), resident((1, D_MODEL)), resident((1, D_MODEL)),
        resident((N_KV_HEADS, Q_PER_KV * WINDOW)),
        pl.BlockSpec(memory_space=pl.ANY), pl.BlockSpec(memory_space=pl.ANY),
        pl.BlockSpec(memory_space=pl.ANY), pl.BlockSpec(memory_space=pl.ANY),
        resident((3, 2 * D_FF)), resident((1, 2 * D_FF)),
        resident((n_steps, LANES)), resident((n_steps, LANES)),
        resident((SEQ_BLOCK, LANES)), resident((SEQ_BLOCK, LANES)),
        resident((N_RET_HEADS, SEQ_BLOCK, SEQ_BLOCK)),
        resident((N_RET_HEADS, SEQ_BLOCK, LANES)), resident((N_RET_HEADS, SEQ_BLOCK, LANES)),
    ]
    n_conv_slabs = 2 * D_FF // LANES
    call = pl.pallas_call(
        functools.partial(_layer_kernel, tabs["chunk_decay"], n_steps),
        grid=(n_steps + 1,),
        in_specs=in_specs,
        out_specs=pl.BlockSpec((SEQ_BLOCK, D_MODEL), lambda i: (jnp.maximum(i - 1, 0), 0)),
        out_shape=jax.ShapeDtypeStruct((seq, D_MODEL), jnp.float32),
        scratch_shapes=[
            pltpu.VMEM((T_ROWS, D_MODEL), bf16),
            pltpu.VMEM((D_MODEL, S_COLS), bf16),
            pltpu.VMEM((D_MODEL, D_MODEL), bf16),
            pltpu.VMEM((D_MODEL, 2 * D_FF), bf16),
            pltpu.VMEM((D_FF, D_MODEL), bf16),
            pltpu.SemaphoreType.DMA((2 * n_conv_slabs,)),
            pltpu.VMEM((WINDOW, KV_W), jnp.float32),
            pltpu.VMEM((KV_W, WINDOW), jnp.float32),
            pltpu.VMEM((N_RET_HEADS, RET_HEAD_DIM, RET_HEAD_DIM), jnp.float32),
            pltpu.VMEM((n_conv_slabs, CONV_ROW0 + 2 * SEQ_BLOCK, LANES), jnp.float32),
            pltpu.VMEM((D_MODEL, SEQ_BLOCK), bf16),
            pltpu.VMEM((SEQ_BLOCK, D_FF), bf16),
            pltpu.VMEM((SEQ_BLOCK, D_MODEL), jnp.float32),
            pltpu.VMEM((SEQ_BLOCK, D_MODEL), bf16),
        ],
        compiler_params=pltpu.CompilerParams(
            dimension_semantics=("arbitrary",), vmem_limit_bytes=VMEM_LIMIT_BYTES),
        name="hybrid_layer",
    )

    outs = []
    for bi in range(batch):
        xb = x[bi]
        for l in range(depth):
            sink_rows = jnp.repeat(attn_sinks[l].reshape(N_KV_HEADS, Q_PER_KV), WINDOW, axis=1)
            xb = call(
                xb, xb,
                mix_pre_norm[l].reshape(1, D_MODEL), mix_post_norm[l].reshape(1, D_MODEL),
                ffn_pre_norm[l].reshape(1, D_MODEL), ffn_post_norm[l].reshape(1, D_MODEL),
                sink_rows,
                w_in[l], w_out[l], w_up[l], w_down[l],
                conv_w[l], conv_b[l].reshape(1, 2 * D_FF),
                tabs["cos_base"], tabs["sin_base"], tabs["cos_intra"], tabs["sin_intra"],
                tabs["decay_t"], tabs["xi"], tabs["zeta"],
            )
        outs.append(xb)
    return outs[0][None] if batch == 1 else jnp.stack(outs, axis=0)
```
